```python
import jax, jax.numpy as jnp
from jax import lax
import numpy as np

D_MODEL = 1024
BATCH = 8
SEQ = 2048
DEPTH = 1
DEC_BATCH = 128
DEC_SEQ = 4
PAST_LEN = 8192
PAGE_SIZE = 128

H_A = 8
D_HEAD_A = 64
D_A = H_A * D_HEAD_A
H_IDX = 4
D_IDX = 64
TOPK_MAX = 256
H_B = 8
D_NOPE = 64
D_ROPE = 32
D_V = 64
D_Q_LORA = 384
D_C = 256
D_B = H_B * D_V
MLA_SCALE = (D_NOPE + D_ROPE) ** -0.5
ROPE_THETA = 10000.0
NORM_EPS = 1e-6
Q_BLOCK = 128
SPLIT_SIZES = (D_A, D_A, D_A, D_A, H_IDX * D_IDX, D_IDX, H_IDX, D_Q_LORA, D_C, D_ROPE, D_B, D_MODEL, D_MODEL)
D_IN = sum(SPLIT_SIZES)

kernel_name = "hybrid_dsa_mla_gated_decoder_step"


def rmsnorm(x, g):
    xf = x.astype(jnp.float32)
    y = xf * lax.rsqrt(jnp.mean(xf * xf, axis=-1, keepdims=True) + NORM_EPS)
    return (y * g.astype(jnp.float32)).astype(x.dtype)


def rope(x, pos):
    d = x.shape[-1]
    inv = ROPE_THETA ** (-jnp.arange(0, d, 2, dtype=jnp.float32) / d)
    ang = pos.astype(jnp.float32)[:, None] * inv[None, :]
    cos = jnp.cos(ang)[:, None, :]
    sin = jnp.sin(ang)[:, None, :]
    xf = x.astype(jnp.float32)
    x1, x2 = xf[..., : d // 2], xf[..., d // 2:]
    return jnp.concatenate([x1 * cos - x2 * sin, x1 * sin + x2 * cos], axis=-1).astype(x.dtype)


def layer_front(x, c, pos, w_ada, b_ada, g_norm, w_in, g_q_lora, w_uq, g_kv, w_ukv):
    B, T, _ = x.shape
    mod = jax.nn.silu(c) @ w_ada + b_ada
    shift, scale, gate = jnp.split(mod, 3, axis=-1)
    h = rmsnorm(x, g_norm) * (1.0 + scale[:, None, :]) + shift[:, None, :]
    proj = h @ w_in
    parts = jnp.split(proj, np.cumsum(SPLIT_SIZES)[:-1].tolist(), axis=-1)
    q_a, k_a, v_a, z_a, q_i, k_i, w_i, q_l, c_kv, k_r, z_b, g_a, g_b = parts
    q_a = rope(q_a.reshape(B, T, H_A, D_HEAD_A), pos)
    k_a = rope(k_a.reshape(B, T, H_A, D_HEAD_A), pos)
    v_a = v_a.reshape(B, T, H_A, D_HEAD_A)
    q_i = rope(q_i.reshape(B, T, H_IDX, D_IDX), pos)
    k_i = rope(k_i[:, :, None, :], pos)[:, :, 0, :]
    w_i = w_i * (H_IDX ** -0.5)
    q_b = (rmsnorm(q_l, g_q_lora) @ w_uq).reshape(B, T, H_B, D_NOPE + D_ROPE)
    q_pe = rope(q_b[..., D_NOPE:], pos)
    q_abs = jnp.einsum('bthd,chd->bthc', q_b[..., :D_NOPE], w_ukv[..., :D_NOPE])
    lat = rmsnorm(c_kv, g_kv)
    k_pe = rope(k_r[:, :, None, :], pos)[:, :, 0, :]
    return (gate, q_a, k_a, v_a, z_a, q_i, k_i, w_i, q_abs, q_pe, lat, k_pe, z_b, g_a, g_b)


def indexer_scores(q_i, w_i, k_i):
    s = jnp.einsum('bthd,bsd->bths', q_i.astype(jnp.float32), k_i.astype(jnp.float32)) * (D_IDX ** -0.5)
    return jnp.einsum('bths,bth->bts', jax.nn.relu(s), w_i.astype(jnp.float32))


def select_topk(scores, valid, k):
    masked = jnp.where(valid, scores, -jnp.inf)
    vals, idx = lax.top_k(masked, k)
    return idx, jnp.isfinite(vals)


def sparse_attend(q, k_sel, v_sel, sel_valid):
    s = jnp.einsum('bthd,btkhd->bhtk', q.astype(jnp.float32), k_sel.astype(jnp.float32)) * (D_HEAD_A ** -0.5)
    s = jnp.where(sel_valid[:, None], s, -jnp.inf)
    p = jax.nn.softmax(s, axis=-1)
    return jnp.einsum('bhtk,btkhd->bthd', p, v_sel.astype(jnp.float32)).astype(q.dtype)


def mla_attend(q_abs, q_pe, lat, k_pe, valid):
    latf = lat.astype(jnp.float32)
    s = (jnp.einsum('bthc,bsc->bhts', q_abs.astype(jnp.float32), latf)
         + jnp.einsum('bthr,bsr->bhts', q_pe.astype(jnp.float32), k_pe.astype(jnp.float32))) * MLA_SCALE
    s = jnp.where(valid, s, -jnp.inf)
    p = jax.nn.softmax(s, axis=-1)
    return jnp.einsum('bhts,bsc->bthc', p, latf).astype(q_abs.dtype)


def layer_back(x, gate, o_a, z_a, o_lat, z_b, g_a, g_b, w_ukv, w_o_a, w_o_b, w_out):
    B, T, _ = x.shape
    y_a = (o_a.reshape(B, T, D_A) * jax.nn.silu(z_a)) @ w_o_a
    o_b = jnp.einsum('bthc,chd->bthd', o_lat, w_ukv[..., D_NOPE:]).reshape(B, T, D_B)
    y_b = (o_b * jax.nn.silu(z_b)) @ w_o_b
    merged = jax.nn.sigmoid(g_a) * y_a + jax.nn.sigmoid(g_b) * y_b
    return x + gate[:, None, :] * (merged @ w_out)


def prompt_attention(q_a, k_a, v_a, q_i, k_i, w_i, q_abs, q_pe, lat, k_pe):
    B, T = q_a.shape[:2]
    n_blk = T // Q_BLOCK
    topk = min(TOPK_MAX, T // 4)
    key_pos = jnp.arange(T)
    gather_rows = jax.vmap(lambda arr, ii: arr[ii])

    def a_block(i):
        t0 = i * Q_BLOCK
        qb = lax.dynamic_slice_in_dim(q_a, t0, Q_BLOCK, axis=1)
        qib = lax.dynamic_slice_in_dim(q_i, t0, Q_BLOCK, axis=1)
        wib = lax.dynamic_slice_in_dim(w_i, t0, Q_BLOCK, axis=1)
        valid = key_pos[None, :] <= (t0 + jnp.arange(Q_BLOCK))[:, None]
        idx, sel_valid = select_topk(indexer_scores(qib, wib, k_i), valid[None], topk)
        return sparse_attend(qb, gather_rows(k_a, idx), gather_rows(v_a, idx), sel_valid)

    def b_block(i):
        t0 = i * Q_BLOCK
        qab = lax.dynamic_slice_in_dim(q_abs, t0, Q_BLOCK, axis=1)
        qpb = lax.dynamic_slice_in_dim(q_pe, t0, Q_BLOCK, axis=1)
        valid = key_pos[None, :] <= (t0 + jnp.arange(Q_BLOCK))[:, None]
        return mla_attend(qab, qpb, lat, k_pe, valid)

    blocks = jnp.arange(n_blk)
    o_a = lax.map(a_block, blocks)
    o_a = jnp.moveaxis(o_a, 0, 1).reshape(B, T, H_A, D_HEAD_A)
    o_lat = lax.map(b_block, blocks)
    o_lat = jnp.moveaxis(o_lat, 0, 1).reshape(B, T, H_B, D_C)
    return o_a, o_lat


def sample_attention(q_a, k_a, v_a, q_i, k_i, w_i, q_abs, q_pe, lat, k_pe,
                     cache_k_a, cache_v_a, cache_idx_k, cache_lat, cache_kpe, page_table):
    DB, T = q_a.shape[:2]
    n_pages = page_table.shape[1]
    past = n_pages * PAGE_SIZE
    L = past + T
    topk = min(TOPK_MAX, L // 4)

    def gather_pages(pool):
        return pool[page_table].reshape((DB, past) + pool.shape[2:])

    valid = jnp.arange(L)[None, :] <= (past + jnp.arange(T))[:, None]
    k_i_all = jnp.concatenate([gather_pages(cache_idx_k), k_i.astype(cache_idx_k.dtype)], axis=1)
    idx, sel_valid = select_topk(indexer_scores(q_i, w_i, k_i_all), valid[None], topk)
    is_past = idx < past
    s_c = jnp.minimum(idx, past - 1)
    phys_page = jax.vmap(lambda pt, j: pt[j])(page_table, s_c // PAGE_SIZE)
    rows = phys_page * PAGE_SIZE + s_c % PAGE_SIZE
    j_new = jnp.clip(idx - past, 0, T - 1)
    gather_rows = jax.vmap(lambda arr, ii: arr[ii])
    k_past = cache_k_a.reshape((-1, H_A, D_HEAD_A))[rows]
    v_past = cache_v_a.reshape((-1, H_A, D_HEAD_A))[rows]
    k_sel = jnp.where(is_past[..., None, None], k_past, gather_rows(k_a, j_new).astype(k_past.dtype))
    v_sel = jnp.where(is_past[..., None, None], v_past, gather_rows(v_a, j_new).astype(v_past.dtype))
    o_a = sparse_attend(q_a, k_sel, v_sel, sel_valid)
    lat_all = jnp.concatenate([gather_pages(cache_lat), lat.astype(cache_lat.dtype)], axis=1)
    kpe_all = jnp.concatenate([gather_pages(cache_kpe), k_pe.astype(cache_kpe.dtype)], axis=1)
    o_lat = mla_attend(q_abs, q_pe, lat_all, kpe_all, valid)
    return o_a, o_lat


def setup_inputs(seed: int = 0) -> dict:
    key = jax.random.key(seed)
    ks = jax.random.split(key, 32)
    n_pages = PAST_LEN // PAGE_SIZE
    n_used = DEC_BATCH * n_pages
    n_pool = n_used + max(1, n_used // 4)

    def nrm(k, shape, s=1.0):
        return s * jax.random.normal(k, shape, jnp.float32)

    page_table = jax.random.permutation(ks[0], n_pool)[:n_used].reshape(DEC_BATCH, n_pages).astype(jnp.int32)
    return {
        "x_prompt": nrm(ks[1], (BATCH, SEQ, D_MODEL)),
        "x_sample": nrm(ks[2], (DEC_BATCH, DEC_SEQ, D_MODEL)),
        "cache_k_a": nrm(ks[3], (n_pool, PAGE_SIZE, H_A, D_HEAD_A)),
        "cache_v_a": nrm(ks[4], (n_pool, PAGE_SIZE, H_A, D_HEAD_A)),
        "cache_idx_k": nrm(ks[5], (n_pool, PAGE_SIZE, D_IDX)),
        "cache_lat": nrm(ks[6], (n_pool, PAGE_SIZE, D_C)),
        "cache_kpe": nrm(ks[7], (n_pool, PAGE_SIZE, D_ROPE)),
        "page_table": page_table,
        "c_prompt": nrm(ks[8], (BATCH, D_MODEL)),
        "c_sample": nrm(ks[9], (DEC_BATCH, D_MODEL)),
        "w_ada": nrm(ks[10], (D_MODEL, 3 * D_MODEL), 0.2 * D_MODEL ** -0.5),
        "b_ada": nrm(ks[11], (3 * D_MODEL,), 0.02),
        "g_norm": 1.0 + nrm(ks[12], (D_MODEL,), 0.05),
        "w_in": nrm(ks[13], (D_MODEL, D_IN), D_MODEL ** -0.5),
        "g_q_lora": 1.0 + nrm(ks[14], (D_Q_LORA,), 0.05),
        "w_uq": nrm(ks[15], (D_Q_LORA, H_B * (D_NOPE + D_ROPE)), D_Q_LORA ** -0.5),
        "g_kv": 1.0 + nrm(ks[16], (D_C,), 0.05),
        "w_ukv": nrm(ks[17], (D_C, H_B, D_NOPE + D_V), D_C ** -0.5),
        "w_o_a": nrm(ks[18], (D_A, D_MODEL), D_A ** -0.5),
        "w_o_b": nrm(ks[19], (D_B, D_MODEL), D_B ** -0.5),
        "w_out": nrm(ks[20], (D_MODEL, D_MODEL), D_MODEL ** -0.5),
        "g_final": 1.0 + nrm(ks[21], (D_MODEL,), 0.05),
    }


def reference(x_prompt, x_sample, cache_k_a, cache_v_a, cache_idx_k, cache_lat, cache_kpe, page_table,
              c_prompt, c_sample, w_ada, b_ada, g_norm, w_in, g_q_lora, w_uq, g_kv, w_ukv,
              w_o_a, w_o_b, w_out, g_final):
    w_front = (w_ada, b_ada, g_norm, w_in, g_q_lora, w_uq, g_kv, w_ukv)
    n_past = page_table.shape[1] * PAGE_SIZE
    pos_p = jnp.arange(x_prompt.shape[1])
    pos_s = n_past + jnp.arange(x_sample.shape[1])

    (gate_p, q_a_p, k_a_p, v_a_p, z_a_p, q_i_p, k_i_p, w_i_p, q_abs_p, q_pe_p, lat_p, k_pe_p,
     z_b_p, g_a_p, g_b_p) = layer_front(x_prompt, c_prompt, pos_p, *w_front)
    o_a_p, o_lat_p = prompt_attention(q_a_p, k_a_p, v_a_p, q_i_p, k_i_p, w_i_p, q_abs_p, q_pe_p, lat_p, k_pe_p)
    h_p = layer_back(x_prompt, gate_p, o_a_p, z_a_p, o_lat_p, z_b_p, g_a_p, g_b_p, w_ukv, w_o_a, w_o_b, w_out)
    y_prompt = rmsnorm(h_p, g_final)

    (gate_s, q_a_s, k_a_s, v_a_s, z_a_s, q_i_s, k_i_s, w_i_s, q_abs_s, q_pe_s, lat_s, k_pe_s,
     z_b_s, g_a_s, g_b_s) = layer_front(x_sample, c_sample, pos_s, *w_front)
    o_a_s, o_lat_s = sample_attention(q_a_s, k_a_s, v_a_s, q_i_s, k_i_s, w_i_s, q_abs_s, q_pe_s, lat_s, k_pe_s,
                                      cache_k_a, cache_v_a, cache_idx_k, cache_lat, cache_kpe, page_table)
    h_s = layer_back(x_sample, gate_s, o_a_s, z_a_s, o_lat_s, z_b_s, g_a_s, g_b_s, w_ukv, w_o_a, w_o_b, w_out)
    y_sample = rmsnorm(h_s, g_final)

    return (y_prompt, y_sample,
            k_a_p, v_a_p, k_i_p, lat_p, k_pe_p,
            k_a_s, v_a_s, k_i_s, lat_s, k_pe_s)
```

```python
import functools

import jax
import jax.numpy as jnp
import numpy as np
from jax import lax
from jax.experimental import pallas as pl
from jax.experimental.pallas import tpu as pltpu

F32 = jnp.float32
BF16 = jnp.bfloat16
I32 = jnp.int32

D_MODEL = 1024
PAGE_SIZE = 128
H_A = 8
D_HEAD_A = 64
D_A = H_A * D_HEAD_A
H_IDX = 4
D_IDX = 64
TOPK_MAX = 256
H_B = 8
D_NOPE = 64
D_ROPE = 32
D_V = 64
D_Q_LORA = 384
D_C = 256
D_B = H_B * D_V
MLA_SCALE = (D_NOPE + D_ROPE) ** -0.5
ROPE_THETA = 10000.0
NORM_EPS = 1e-6
SPLIT_SIZES = (D_A, D_A, D_A, D_A, H_IDX * D_IDX, D_IDX, H_IDX, D_Q_LORA, D_C, D_ROPE, D_B, D_MODEL, D_MODEL)

LANES = 128
KEY_CHUNK = 128
INT_MIN = -(2 ** 31)
NEG_BIG = -1e30
VMEM_LIMIT = 56 * 1024 * 1024

C_QA, C_KA, C_QAR, C_KAR, C_VA, C_ZA = 0, 512, 1024, 1536, 2048, 2560
C_QI, C_QIR = 3072, 3328
C_SM, C_SMR = 3584, 3712
C_KR, C_KRR = 3840, 3968
C_KI, C_KIR = 4096, 4224
C_QL, C_CKV, C_ZB, C_GA, C_GB = 4352, 4736, 4992, 5504, 6528
N_EXT = 7552
T_CA, T_SA, T_CS, T_SS, T_CP, T_SP, T_W = 0, 512, 1024, 1152, 1280, 1536, 1792


def _cparams(sem):
    return pltpu.CompilerParams(dimension_semantics=sem, vmem_limit_bytes=VMEM_LIMIT)


def _sigmoid(x):
    return 1.0 / (1.0 + jnp.exp(-x))


def _dot(a, b):
    return jnp.dot(a, b, preferred_element_type=F32)


def _dot_nt(a, b):
    return lax.dot_general(a, b, (((1,), (1,)), ((), ())), preferred_element_type=F32)


def _sortable(x):
    bits = lax.bitcast_convert_type(x + 0.0, I32)
    return bits ^ ((bits >> 31) & 0x7FFFFFFF)


def _mod_kernel(c_ref, w_ref, b_ref, o_ref):
    c = c_ref[...]
    s = (c * _sigmoid(c)).astype(BF16)
    o_ref[...] = _dot(s, w_ref[...].astype(BF16)) + b_ref[...]


def _mod(c_all, w_ada, b_ada):
    r = c_all.shape[0]
    nb = 3
    return pl.pallas_call(
        _mod_kernel,
        grid=(nb,),
        in_specs=[pl.BlockSpec((r, D_MODEL), lambda j: (0, 0)),
                  pl.BlockSpec((D_MODEL, D_MODEL), lambda j: (0, j)),
                  pl.BlockSpec((1, D_MODEL), lambda j: (0, j))],
        out_specs=pl.BlockSpec((r, D_MODEL), lambda j: (0, j)),
        out_shape=jax.ShapeDtypeStruct((r, 3 * D_MODEL), F32),
        compiler_params=_cparams(("arbitrary",)),
        name="mod",
    )(c_all, w_ada, b_ada.reshape(1, -1))


def _front_kernel(x_ref, mod_ref, tab_ref, gn_ref, gq_ref, gkv_ref, win_ref, wuq_ref,
                  qa_ref, ka_ref, kab_ref, va_ref, vab_ref, za_ref, qi_ref, sm_ref, krr_ref, kir_ref,
                  qn_ref, qp_ref, lat_ref, latb_ref, zb_ref, ga_ref, gb_ref):
    x = x_ref[...]
    xn = x * lax.rsqrt(jnp.mean(x * x, axis=-1, keepdims=True) + NORM_EPS) * gn_ref[...]
    mod = mod_ref[...]
    h = (xn * (1.0 + mod[:, D_MODEL:2 * D_MODEL]) + mod[:, :D_MODEL]).astype(BF16)

    def proj(lo, hi):
        return _dot(h, win_ref[:, lo:hi])

    cos_a = tab_ref[:, T_CA:T_CA + 512]
    sin_a = tab_ref[:, T_SA:T_SA + 512]
    a = proj(C_QA, C_VA)
    qa_ref[...] = (a[:, 0:512] * cos_a + a[:, 1024:1536] * sin_a).astype(BF16)
    ka = a[:, 512:1024] * cos_a + a[:, 1536:2048] * sin_a
    ka_ref[...] = ka
    kab_ref[...] = ka.astype(BF16)
    a = proj(C_VA, C_QI)
    va_ref[...] = a[:, 0:512]
    vab_ref[...] = a[:, 0:512].astype(BF16)
    za_ref[...] = a[:, 512:1024]
    a = proj(C_QI, C_QL)
    qi_ref[...] = (a[:, 0:256] * cos_a[:, 0:256] + a[:, 256:512] * sin_a[:, 0:256]).astype(BF16)
    sm_ref[...] = a[:, 512:640] * tab_ref[:, T_CS:T_CS + 128] + a[:, 640:768] * tab_ref[:, T_SS:T_SS + 128]
    cos_p = tab_ref[:, T_CP:T_CP + 256]
    sin_p = tab_ref[:, T_SP:T_SP + 256]
    krr_ref[...] = (a[:, 768:896] * cos_p[:, 0:128] + a[:, 896:1024] * sin_p[:, 0:128]).astype(BF16)
    kir_ref[...] = (a[:, 1024:1152] * cos_a[:, 0:128] + a[:, 1152:1280] * sin_a[:, 0:128]).astype(BF16)
    a = proj(C_QL, C_ZB)
    ql = a[:, 0:D_Q_LORA]
    qln = ql * lax.rsqrt(jnp.mean(ql * ql, axis=-1, keepdims=True) + NORM_EPS) * gq_ref[...]
    qb = _dot(qln.astype(BF16), wuq_ref[...])
    qn_ref[...] = qb[:, 0:512].astype(BF16)
    qp_ref[...] = (qb[:, 512:768] * cos_p + qb[:, 768:1024] * sin_p).astype(BF16)
    ckv = a[:, D_Q_LORA:D_Q_LORA + D_C]
    lat = ckv * lax.rsqrt(jnp.mean(ckv * ckv, axis=-1, keepdims=True) + NORM_EPS) * gkv_ref[...]
    lat_ref[...] = lat
    latb_ref[...] = lat.astype(BF16)
    a = proj(C_ZB, C_GB)
    zb_ref[...] = a[:, 0:512]
    ga_ref[...] = a[:, 512:1536]
    gb_ref[...] = proj(C_GB, N_EXT)


_FRONT_OUT = (
    (512, BF16), (512, F32), (512, BF16), (512, F32), (512, BF16), (512, F32), (256, BF16), (128, F32),
    (128, BF16), (128, BF16), (512, BF16), (256, BF16), (256, F32), (256, BF16), (512, F32), (1024, F32),
    (1024, F32))


def _front(x, mod, tab, gn, gq, gkv, win, wuq, tm, per_row_mod):
    g, t, _ = x.shape
    nt = t // tm
    const = lambda b, i: (0, 0)
    if per_row_mod:
        mod_spec = pl.BlockSpec((None, tm, 3 * D_MODEL), lambda b, i: (b, i, 0))
    else:
        mod_spec = pl.BlockSpec((None, 1, 3 * D_MODEL), lambda b, i: (b, 0, 0))
    in_specs = [
        pl.BlockSpec((None, tm, D_MODEL), lambda b, i: (b, i, 0)),
        mod_spec,
        pl.BlockSpec((tm, T_W), lambda b, i: (i, 0)),
        pl.BlockSpec((1, D_MODEL), const),
        pl.BlockSpec((1, D_Q_LORA), const),
        pl.BlockSpec((1, D_C), const),
        pl.BlockSpec((D_MODEL, N_EXT), const),
        pl.BlockSpec((D_Q_LORA, 1024), const),
    ]
    out_specs = [pl.BlockSpec((None, tm, w), lambda b, i: (b, i, 0)) for w, _ in _FRONT_OUT]
    out_shape = [jax.ShapeDtypeStruct((g, t, w), dt) for w, dt in _FRONT_OUT]
    return pl.pallas_call(
        _front_kernel, grid=(g, nt), in_specs=in_specs, out_specs=out_specs, out_shape=out_shape,
        compiler_params=_cparams(("arbitrary", "arbitrary")), name="front",
    )(x, mod, tab, gn, gq, gkv, win, wuq)


def _select(count_fn, rows, k_sel, idx_bits):
    kf = float(k_sel)

    def bit_step(it, t):
        cand = jnp.where(it == 0, jnp.zeros_like(t), t + lax.shift_left(jnp.int32(1), 31 - it))
        cnt = count_fn(lambda key, idx: key >= cand)
        return jnp.where(cnt >= kf, cand, t)

    t = lax.fori_loop(0, 32, bit_step, jnp.full((rows, LANES), INT_MIN, I32))
    need = kf - count_fn(lambda key, idx: key > t)

    def tie_step(it, p):
        cand = p + lax.shift_left(jnp.int32(1), idx_bits - 1 - it)
        cnt = count_fn(lambda key, idx: (key == t) & (idx < cand))
        return jnp.where(cnt < need, cand, p)

    p = lax.fori_loop(0, idx_bits, tie_step, jnp.zeros((rows, LANES), I32))
    return t, p


def _selected(key, idx, t, p):
    return ((key > t) | ((key == t) & (idx <= p))) & (key != INT_MIN)


def _dsa_p_kernel(qi_ref, sm_ref, kir_ref, qa_ref, kab_ref, vab_ref, oa_ref,
                  keys_ref, qs_ref, acc_ref, m_ref, l_ref, *, tq, k_sel, idx_bits):
    i = pl.program_id(1)
    nchunk = i + 1
    lane = lax.broadcasted_iota(I32, (tq, LANES), 1)
    row = lax.broadcasted_iota(I32, (tq, LANES), 0)
    lo = lane < 64

    qi = qi_ref[...]
    zero = jnp.zeros((tq, LANES), BF16)
    parts = []
    for j in range(2):
        col = qi[:, j * LANES:(j + 1) * LANES]
        parts += [jnp.where(lo, col, zero), jnp.where(lo, zero, col)]
    qi_stack = jnp.concatenate(parts, axis=0)
    sm = sm_ref[...]
    w_heads = [sm[:, 96 + hh:97 + hh] * (D_IDX ** -0.5) for hh in range(H_IDX)]

    def score_chunk(c, carry):
        kc = kir_ref[pl.ds(pl.multiple_of(c * KEY_CHUNK, KEY_CHUNK), KEY_CHUNK), :]
        s = _dot_nt(qi_stack, kc)
        tot = jnp.zeros((tq, LANES), F32)
        for hh in range(H_IDX):
            tot = tot + jnp.maximum(s[hh * tq:(hh + 1) * tq], 0.0) * w_heads[hh]
        key = _sortable(tot)
        valid = (c * KEY_CHUNK + lane) <= (i * tq + row)
        keys_ref[c] = jnp.where(valid, key, INT_MIN)
        return carry

    lax.fori_loop(0, nchunk, score_chunk, 0)

    def count_fn(pred):
        def body(c, acc):
            hit = pred(keys_ref[c], c * KEY_CHUNK + lane)
            return acc + jnp.where(hit, 1.0, 0.0)
        acc = lax.fori_loop(0, nchunk, body, jnp.zeros((tq, LANES), F32))
        return jnp.sum(acc, axis=-1, keepdims=True)

    t, p = _select(count_fn, tq, k_sel, idx_bits)

    qa = qa_ref[...] * (D_HEAD_A ** -0.5)
    for j in range(4):
        col = qa[:, j * LANES:(j + 1) * LANES]
        qs_ref[j] = jnp.concatenate([jnp.where(lo, col, zero), jnp.where(lo, zero, col)], axis=0)
    acc_ref[...] = jnp.zeros_like(acc_ref)
    m_ref[...] = jnp.full_like(m_ref, NEG_BIG)
    l_ref[...] = jnp.zeros_like(l_ref)

    def attend_chunk(c, carry):
        off = pl.multiple_of(c * KEY_CHUNK, KEY_CHUNK)
        sel = _selected(keys_ref[c], c * KEY_CHUNK + lane, t, p)
        for j in range(4):
            kc = kab_ref[pl.ds(off, KEY_CHUNK), j * LANES:(j + 1) * LANES]
            vc = vab_ref[pl.ds(off, KEY_CHUNK), j * LANES:(j + 1) * LANES]
            s2 = _dot_nt(qs_ref[j], kc)
            prs, alphas = [], []
            for half in range(2):
                hh = 2 * j + half
                s = s2[half * tq:(half + 1) * tq]
                m_old = m_ref[hh]
                m_new = jnp.maximum(m_old, jnp.max(jnp.where(sel, s, NEG_BIG), axis=-1, keepdims=True))
                pr = jnp.where(sel, jnp.exp(s - m_new), 0.0)
                alpha = jnp.exp(m_old - m_new)
                l_ref[hh] = alpha * l_ref[hh] + jnp.sum(pr, axis=-1, keepdims=True)
                m_ref[hh] = m_new
                prs.append(pr.astype(BF16))
                alphas.append(alpha)
            pv = _dot(jnp.concatenate(prs, axis=0), vc)
            comb = jnp.where(lo, pv[:tq], pv[tq:])
            al = jnp.where(lo, alphas[0], alphas[1])
            acc_ref[:, j * LANES:(j + 1) * LANES] = acc_ref[:, j * LANES:(j + 1) * LANES] * al + comb
        return carry

    lax.fori_loop(0, nchunk, attend_chunk, 0)
    for j in range(4):
        inv = jnp.where(lo, 1.0 / l_ref[2 * j], 1.0 / l_ref[2 * j + 1])
        oa_ref[:, j * LANES:(j + 1) * LANES] = acc_ref[:, j * LANES:(j + 1) * LANES] * inv


def _dsa_p(qi, sm, kir, qa, kab, vab, k_sel):
    b, t, _ = qa.shape
    tq = KEY_CHUNK
    nq = t // tq
    idx_bits = max(1, int(np.ceil(np.log2(t))))
    blk = lambda w: pl.BlockSpec((None, tq, w), lambda bb, i: (bb, i, 0))
    full = lambda w: pl.BlockSpec((None, t, w), lambda bb, i: (bb, 0, 0))
    kern = functools.partial(_dsa_p_kernel, tq=tq, k_sel=k_sel, idx_bits=idx_bits)
    return pl.pallas_call(
        kern, grid=(b, nq),
        in_specs=[blk(256), blk(128), full(128), blk(512), full(512), full(512)],
        out_specs=blk(512),
        out_shape=jax.ShapeDtypeStruct((b, t, D_A), F32),
        scratch_shapes=[pltpu.VMEM((nq, tq, LANES), I32),
                        pltpu.VMEM((4, 2 * tq, LANES), BF16),
                        pltpu.VMEM((tq, D_A), F32),
                        pltpu.VMEM((H_A, tq, 1), F32),
                        pltpu.VMEM((H_A, tq, 1), F32)],
        compiler_params=_cparams(("arbitrary", "arbitrary")), name="dsa_p",
    )(qi, sm, kir, qa, kab, vab)


def _mla_p_kernel(qn_ref, qp_ref, latb_ref, krr_ref, wkt_ref, wvp_ref, ob_ref,
                  qabs_ref, qpe_ref, acc_ref, m_ref, l_ref, *, tq, tk):
    i = pl.program_id(1)
    nchunk = ((i + 1) * tq + tk - 1) // tk
    lane = lax.broadcasted_iota(I32, (tq, LANES), 1)
    zero = jnp.zeros((tq, LANES), BF16)
    qn = qn_ref[...]
    qp = qp_ref[...]
    for hh in range(H_B):
        j, half = hh // 2, hh % 2
        col = qn[:, j * LANES:(j + 1) * LANES]
        masked = jnp.where((lane // 64) == half, col, zero)
        qabs_ref[hh * tq:(hh + 1) * tq, :] = _dot(masked, wkt_ref[j * LANES:(j + 1) * LANES, :]).astype(BF16)
        j4, quarter = hh // 4, hh % 4
        colp = qp[:, j4 * LANES:(j4 + 1) * LANES]
        qpe_ref[hh * tq:(hh + 1) * tq, :] = jnp.where((lane // 32) == quarter, colp, zero)
    acc_ref[...] = jnp.zeros_like(acc_ref)
    m_ref[...] = jnp.full_like(m_ref, NEG_BIG)
    l_ref[...] = jnp.zeros_like(l_ref)
    rows = H_B * tq
    qpos = i * tq + (lax.broadcasted_iota(I32, (rows, tk), 0) & (tq - 1))
    kiota = lax.broadcasted_iota(I32, (rows, tk), 1)

    def chunk(c, carry):
        off = pl.multiple_of(c * tk, tk)
        latc = latb_ref[pl.ds(off, tk), :]
        krc = krr_ref[pl.ds(off, tk), :]
        s = (_dot_nt(qabs_ref[...], latc) + _dot_nt(qpe_ref[...], krc)) * MLA_SCALE
        valid = (c * tk + kiota) <= qpos
        m_old = m_ref[...]
        m_new = jnp.maximum(m_old, jnp.max(jnp.where(valid, s, NEG_BIG), axis=-1, keepdims=True))
        pr = jnp.where(valid, jnp.exp(s - m_new), 0.0)
        alpha = jnp.exp(m_old - m_new)
        l_ref[...] = alpha * l_ref[...] + jnp.sum(pr, axis=-1, keepdims=True)
        m_ref[...] = m_new
        acc_ref[...] = acc_ref[...] * alpha + _dot(pr.astype(BF16), latc)
        return carry

    lax.fori_loop(0, nchunk, chunk, 0)
    o_lat = (acc_ref[...] * (1.0 / l_ref[...])).astype(BF16)
    for j in range(4):
        ob_ref[:, j * LANES:(j + 1) * LANES] = (
            _dot(o_lat[(2 * j) * tq:(2 * j + 1) * tq], wvp_ref[2 * j])
            + _dot(o_lat[(2 * j + 1) * tq:(2 * j + 2) * tq], wvp_ref[2 * j + 1]))


def _mla_p(qn, qp, latb, krr, wkt, wvp):
    b, t, _ = qn.shape
    tq = 128
    tk = 256 if t % 256 == 0 else 128
    nq = t // tq
    blk = lambda w: pl.BlockSpec((None, tq, w), lambda bb, i: (bb, i, 0))
    full = lambda w: pl.BlockSpec((None, t, w), lambda bb, i: (bb, 0, 0))
    kern = functools.partial(_mla_p_kernel, tq=tq, tk=tk)
    return pl.pallas_call(
        kern, grid=(b, nq),
        in_specs=[blk(512), blk(256), full(256), full(128),
                  pl.BlockSpec((D_A, D_C), lambda bb, i: (0, 0)),
                  pl.BlockSpec((H_B, D_C, LANES), lambda bb, i: (0, 0, 0))],
        out_specs=blk(512),
        out_shape=jax.ShapeDtypeStruct((b, t, D_B), F32),
        scratch_shapes=[pltpu.VMEM((H_B * tq, D_C), BF16),
                        pltpu.VMEM((H_B * tq, LANES), BF16),
                        pltpu.VMEM((H_B * tq, D_C), F32),
                        pltpu.VMEM((H_B * tq, 1), F32),
                        pltpu.VMEM((H_B * tq, 1), F32)],
        compiler_params=_cparams(("arbitrary", "arbitrary")), name="mla_p",
    )(qn, qp, latb, krr, wkt, wvp)


ROWS_S = 8


def _sel_s_kernel(pt_ref, q_ref, w_ref, knew_ref, *rest, n_tok, gpages, n_pages, k_sel, idx_bits):
    pages = rest[:gpages]
    mask_ref = rest[gpages]
    keys_ref = rest[gpages + 1]
    s_idx = pl.program_id(1)
    nsteps = n_pages // gpages
    lane = lax.broadcasted_iota(I32, (ROWS_S, LANES), 1)
    row = lax.broadcasted_iota(I32, (ROWS_S, LANES), 0)
    q = q_ref[...]
    w = w_ref[...]

    def scores(kc):
        s = _dot_nt(q, kc.astype(BF16))
        tot = jnp.zeros((ROWS_S, LANES), F32)
        for hh in range(H_IDX):
            tot = tot + jnp.maximum(s[hh * ROWS_S:(hh + 1) * ROWS_S], 0.0) * w[hh * ROWS_S:(hh + 1) * ROWS_S]
        return _sortable(tot)

    for g in range(gpages):
        key = scores(pages[g][...])
        keys_ref[s_idx * gpages + g] = jnp.where(row < n_tok, key, INT_MIN)

    @pl.when(s_idx == nsteps - 1)
    def _():
        key = scores(knew_ref[...])
        valid = (lane <= row) & (row < n_tok)
        keys_ref[n_pages] = jnp.where(valid, key, INT_MIN)
        nchunk = n_pages + 1

        def count_fn(pred):
            def body(c, acc):
                hit = pred(keys_ref[c], c * KEY_CHUNK + lane)
                return acc + jnp.where(hit, 1.0, 0.0)
            acc = lax.fori_loop(0, nchunk, body, jnp.zeros((ROWS_S, LANES), F32))
            return jnp.sum(acc, axis=-1, keepdims=True)

        t, p = _select(count_fn, ROWS_S, k_sel, idx_bits)

        def write(c, carry):
            sel = _selected(keys_ref[c], c * KEY_CHUNK + lane, t, p)
            mask_ref[c] = jnp.where(sel, 1.0, 0.0)
            return carry

        lax.fori_loop(0, nchunk, write, 0)


def _sel_s(page_table, q_stack, w_stack, k_new, cache_idx_k, n_tok, k_sel, gpages):
    db, n_pages = page_table.shape
    idx_bits = int(np.ceil(np.log2((n_pages + 1) * KEY_CHUNK)))
    nsteps = n_pages // gpages
    page_specs = [
        pl.BlockSpec((None, PAGE_SIZE, D_IDX), functools.partial(
            lambda bb, s, pt, g: (pt[bb, s * gpages + g], 0, 0), g=g))
        for g in range(gpages)]
    kern = functools.partial(_sel_s_kernel, n_tok=n_tok, gpages=gpages, n_pages=n_pages,
                             k_sel=k_sel, idx_bits=idx_bits)
    grid_spec = pltpu.PrefetchScalarGridSpec(
        num_scalar_prefetch=1, grid=(db, nsteps),
        in_specs=[pl.BlockSpec((None, H_IDX * ROWS_S, D_IDX), lambda bb, s, pt: (bb, 0, 0)),
                  pl.BlockSpec((None, H_IDX * ROWS_S, LANES), lambda bb, s, pt: (bb, 0, 0)),
                  pl.BlockSpec((None, KEY_CHUNK, D_IDX), lambda bb, s, pt: (bb, 0, 0))] + page_specs,
        out_specs=pl.BlockSpec((None, n_pages + 1, ROWS_S, LANES), lambda bb, s, pt: (bb, 0, 0, 0)),
        scratch_shapes=[pltpu.VMEM((n_pages + 1, ROWS_S, LANES), I32)])
    return pl.pallas_call(
        kern, grid_spec=grid_spec,
        out_shape=jax.ShapeDtypeStruct((db, n_pages + 1, ROWS_S, LANES), F32),
        compiler_params=_cparams(("arbitrary", "arbitrary")), name="sel_s",
    )(page_table, q_stack, w_stack, k_new, *([cache_idx_k] * gpages))


def _dsa_s_kernel(pt_ref, qa_ref, mask_ref, knew_ref, vnew_ref, *rest, gpages, n_pages):
    kpages = rest[:gpages]
    vpages = rest[gpages:2 * gpages]
    oa_ref = rest[2 * gpages]
    qs_ref, acc_ref, m_ref, l_ref = rest[2 * gpages + 1:]
    s_idx = pl.program_id(1)
    nsteps = n_pages // gpages
    lane = lax.broadcasted_iota(I32, (ROWS_S, LANES), 1)
    lo = lane < 64

    @pl.when(s_idx == 0)
    def _():
        qa = qa_ref[...] * (D_HEAD_A ** -0.5)
        zero = jnp.zeros((ROWS_S, LANES), BF16)
        for j in range(4):
            col = qa[:, j * LANES:(j + 1) * LANES]
            qs_ref[j] = jnp.concatenate([jnp.where(lo, col, zero), jnp.where(lo, zero, col)], axis=0)
        acc_ref[...] = jnp.zeros_like(acc_ref)
        m_ref[...] = jnp.full_like(m_ref, NEG_BIG)
        l_ref[...] = jnp.zeros_like(l_ref)

    def attend(kc_full, vc_full, msk):
        sel = msk > 0.5
        for j in range(4):
            kc = kc_full[:, j * LANES:(j + 1) * LANES].astype(BF16)
            vc = vc_full[:, j * LANES:(j + 1) * LANES].astype(BF16)
            s2 = _dot_nt(qs_ref[j], kc)
            prs, alphas = [], []
            for half in range(2):
                hh = 2 * j + half
                s = s2[half * ROWS_S:(half + 1) * ROWS_S]
                m_old = m_ref[hh]
                m_new = jnp.maximum(m_old, jnp.max(jnp.where(sel, s, NEG_BIG), axis=-1, keepdims=True))
                pr = jnp.where(sel, jnp.exp(s - m_new), 0.0)
                alpha = jnp.exp(m_old - m_new)
                l_ref[hh] = alpha * l_ref[hh] + jnp.sum(pr, axis=-1, keepdims=True)
                m_ref[hh] = m_new
                prs.append(pr.astype(BF16))
                alphas.append(alpha)
            pv = _dot(jnp.concatenate(prs, axis=0), vc)
            comb = jnp.where(lo, pv[:ROWS_S], pv[ROWS_S:])
            al = jnp.where(lo, alphas[0], alphas[1])
            acc_ref[:, j * LANES:(j + 1) * LANES] = acc_ref[:, j * LANES:(j + 1) * LANES] * al + comb

    for g in range(gpages):
        attend(kpages[g][...], vpages[g][...], mask_ref[s_idx * gpages + g])

    @pl.when(s_idx == nsteps - 1)
    def _():
        attend(knew_ref[...], vnew_ref[...], mask_ref[n_pages])
        for j in range(4):
            inv = jnp.where(lo, 1.0 / jnp.maximum(l_ref[2 * j], 1e-30), 1.0 / jnp.maximum(l_ref[2 * j + 1], 1e-30))
            oa_ref[:, j * LANES:(j + 1) * LANES] = acc_ref[:, j * LANES:(j + 1) * LANES] * inv


def _dsa_s(page_table, qa_pad, mask, k_new, v_new, cache_k, cache_v, gpages):
    db, n_pages = page_table.shape
    nsteps = n_pages // gpages
    page_spec = lambda g: pl.BlockSpec((None, PAGE_SIZE, D_A), functools.partial(
        lambda bb, s, pt, g: (pt[bb, s * gpages + g], 0, 0), g=g))
    per_b3 = lambda r, w: pl.BlockSpec((None, r, w), lambda bb, s, pt: (bb, 0, 0))
    kern = functools.partial(_dsa_s_kernel, gpages=gpages, n_pages=n_pages)
    grid_spec = pltpu.PrefetchScalarGridSpec(
        num_scalar_prefetch=1, grid=(db, nsteps),
        in_specs=[per_b3(ROWS_S, D_A),
                  pl.BlockSpec((None, n_pages + 1, ROWS_S, LANES), lambda bb, s, pt: (bb, 0, 0, 0)),
                  per_b3(KEY_CHUNK, D_A), per_b3(KEY_CHUNK, D_A)]
                 + [page_spec(g) for g in range(gpages)] + [page_spec(g) for g in range(gpages)],
        out_specs=per_b3(ROWS_S, D_A),
        scratch_shapes=[pltpu.VMEM((4, 2 * ROWS_S, LANES), BF16),
                        pltpu.VMEM((ROWS_S, D_A), F32),
                        pltpu.VMEM((H_A, ROWS_S, 1), F32),
                        pltpu.VMEM((H_A, ROWS_S, 1), F32)])
    return pl.pallas_call(
        kern, grid_spec=grid_spec,
        out_shape=jax.ShapeDtypeStruct((db, ROWS_S, D_A), F32),
        compiler_params=_cparams(("arbitrary", "arbitrary")), name="dsa_s",
    )(page_table, qa_pad, mask, k_new, v_new, *([cache_k] * gpages), *([cache_v] * gpages))


def _mla_s_kernel(pt_ref, qn_ref, qpe_ref, latnew_ref, kpenew_ref, wkt_ref, wvp_ref, *rest,
                  gpages, n_pages, n_tok):
    lpages = rest[:gpages]
    ppages = rest[gpages:2 * gpages]
    ob_ref = rest[2 * gpages]
    qabs_ref, acc_ref, m_ref, l_ref = rest[2 * gpages + 1:]
    s_idx = pl.program_id(1)
    nsteps = n_pages // gpages
    rows = H_B * ROWS_S

    @pl.when(s_idx == 0)
    def _():
        qn = qn_ref[...]
        rep = jnp.concatenate([qn] * H_B, axis=0)
        lane = lax.broadcasted_iota(I32, (rows, D_A), 1)
        r = lax.broadcasted_iota(I32, (rows, D_A), 0)
        blk = jnp.where((lane // D_NOPE) == (r // ROWS_S), rep, jnp.zeros_like(rep))
        qabs_ref[...] = _dot(blk, wkt_ref[...]).astype(BF16)
        acc_ref[...] = jnp.zeros_like(acc_ref)
        m_ref[...] = jnp.full_like(m_ref, NEG_BIG)
        l_ref[...] = jnp.zeros_like(l_ref)

    def attend(latc, kpec, valid):
        latb = latc.astype(BF16)
        s = (_dot_nt(qabs_ref[...], latb) + _dot_nt(qpe_ref[...], kpec.astype(BF16))) * MLA_SCALE
        if valid is not None:
            s_m = jnp.where(valid, s, NEG_BIG)
        else:
            s_m = s
        m_old = m_ref[...]
        m_new = jnp.maximum(m_old, jnp.max(s_m, axis=-1, keepdims=True))
        pr = jnp.exp(s - m_new)
        if valid is not None:
            pr = jnp.where(valid, pr, 0.0)
        alpha = jnp.exp(m_old - m_new)
        l_ref[...] = alpha * l_ref[...] + jnp.sum(pr, axis=-1, keepdims=True)
        m_ref[...] = m_new
        acc_ref[...] = acc_ref[...] * alpha + _dot(pr.astype(BF16), latb)

    for g in range(gpages):
        attend(lpages[g][...], ppages[g][...], None)

    @pl.when(s_idx == nsteps - 1)
    def _():
        kk = lax.broadcasted_iota(I32, (rows, KEY_CHUNK), 1)
        tok = lax.broadcasted_iota(I32, (rows, KEY_CHUNK), 0) & (ROWS_S - 1)
        attend(latnew_ref[...], kpenew_ref[...], (kk <= tok) & (kk < n_tok))
        o_lat = (acc_ref[...] * (1.0 / l_ref[...])).astype(BF16)
        for j in range(4):
            ob_ref[:, j * LANES:(j + 1) * LANES] = (
                _dot(o_lat[(2 * j) * ROWS_S:(2 * j + 1) * ROWS_S], wvp_ref[2 * j])
                + _dot(o_lat[(2 * j + 1) * ROWS_S:(2 * j + 2) * ROWS_S], wvp_ref[2 * j + 1]))


def _mla_s(page_table, qn_pad, qpe_rows, lat_new, kpe_new, wkt, wvp, cache_lat, cache_kpe, n_tok, gpages):
    db, n_pages = page_table.shape
    nsteps = n_pages // gpages
    rows = H_B * ROWS_S
    lspec = lambda g: pl.BlockSpec((None, PAGE_SIZE, D_C), functools.partial(
        lambda bb, s, pt, g: (pt[bb, s * gpages + g], 0, 0), g=g))
    pspec = lambda g: pl.BlockSpec((None, PAGE_SIZE, D_ROPE), functools.partial(
        lambda bb, s, pt, g: (pt[bb, s * gpages + g], 0, 0), g=g))
    per_b3 = lambda r, w: pl.BlockSpec((None, r, w), lambda bb, s, pt: (bb, 0, 0))
    kern = functools.partial(_mla_s_kernel, gpages=gpages, n_pages=n_pages, n_tok=n_tok)
    grid_spec = pltpu.PrefetchScalarGridSpec(
        num_scalar_prefetch=1, grid=(db, nsteps),
        in_specs=[per_b3(ROWS_S, D_A), per_b3(rows, D_ROPE), per_b3(KEY_CHUNK, D_C), per_b3(KEY_CHUNK, D_ROPE),
                  pl.BlockSpec((D_A, D_C), lambda bb, s, pt: (0, 0)),
                  pl.BlockSpec((H_B, D_C, LANES), lambda bb, s, pt: (0, 0, 0))]
                 + [lspec(g) for g in range(gpages)] + [pspec(g) for g in range(gpages)],
        out_specs=per_b3(ROWS_S, D_B),
        scratch_shapes=[pltpu.VMEM((rows, D_C), BF16),
                        pltpu.VMEM((rows, D_C), F32),
                        pltpu.VMEM((rows, 1), F32),
                        pltpu.VMEM((rows, 1), F32)])
    return pl.pallas_call(
        kern, grid_spec=grid_spec,
        out_shape=jax.ShapeDtypeStruct((db, ROWS_S, D_B), F32),
        compiler_params=_cparams(("arbitrary", "arbitrary")), name="mla_s",
    )(page_table, qn_pad, qpe_rows, lat_new, kpe_new, wkt, wvp,
      *([cache_lat] * gpages), *([cache_kpe] * gpages))


def _back_kernel(x_ref, mod_ref, oa_ref, za_ref, ob_ref, zb_ref, ga_ref, gb_ref,
                 woa_ref, wob_ref, wout_ref, gf_ref, y_ref):
    za = za_ref[...]
    zb = zb_ref[...]
    ya = _dot((oa_ref[...] * (za * _sigmoid(za))).astype(BF16), woa_ref[...])
    yb = _dot((ob_ref[...] * (zb * _sigmoid(zb))).astype(BF16), wob_ref[...])
    merged = _sigmoid(ga_ref[...]) * ya + _sigmoid(gb_ref[...]) * yb
    gate = mod_ref[...][:, 2 * D_MODEL:]
    h = x_ref[...] + gate * _dot(merged.astype(BF16), wout_ref[...])
    y_ref[...] = h * lax.rsqrt(jnp.mean(h * h, axis=-1, keepdims=True) + NORM_EPS) * gf_ref[...]


def _back(x, mod, oa, za, ob, zb, ga, gb, woa, wob, wout, gf, tm, per_row_mod):
    g, t, _ = x.shape
    nt = t // tm
    const = lambda b, i: (0, 0)
    blk = lambda w: pl.BlockSpec((None, tm, w), lambda b, i: (b, i, 0))
    if per_row_mod:
        mod_spec = pl.BlockSpec((None, tm, 3 * D_MODEL), lambda b, i: (b, i, 0))
    else:
        mod_spec = pl.BlockSpec((None, 1, 3 * D_MODEL), lambda b, i: (b, 0, 0))
    return pl.pallas_call(
        _back_kernel, grid=(g, nt),
        in_specs=[blk(D_MODEL), mod_spec, blk(512), blk(512), blk(512), blk(512), blk(D_MODEL), blk(D_MODEL),
                  pl.BlockSpec((D_A, D_MODEL), const), pl.BlockSpec((D_B, D_MODEL), const),
                  pl.BlockSpec((D_MODEL, D_MODEL), const), pl.BlockSpec((1, D_MODEL), const)],
        out_specs=blk(D_MODEL),
        out_shape=jax.ShapeDtypeStruct((g, t, D_MODEL), F32),
        compiler_params=_cparams(("arbitrary", "arbitrary")), name="back",
    )(x, mod, oa, za, ob, zb, ga, gb, woa, wob, wout, gf)


def _rot_cols(w, n_heads, d):
    k = w.shape[0]
    w4 = w.reshape(k, n_heads, 2, d // 2)
    return jnp.concatenate([-w4[:, :, 1], w4[:, :, 0]], axis=-1).reshape(k, n_heads * d)


def _prep_w_in(w_in):
    offs = np.concatenate([[0], np.cumsum(SPLIT_SIZES)])
    seg = [w_in[:, offs[n]:offs[n + 1]] for n in range(len(SPLIT_SIZES))]
    w_qa, w_ka, w_va, w_za, w_qi, w_ki, w_wi, w_ql, w_ckv, w_kr, w_zb, w_ga, w_gb = seg
    k = w_in.shape[0]
    z = lambda n: jnp.zeros((k, n), w_in.dtype)
    ki_r = _rot_cols(w_ki, 1, D_IDX)
    kr_r = _rot_cols(w_kr, 1, D_ROPE)
    cols = [w_qa, w_ka, _rot_cols(w_qa, H_A, D_HEAD_A), _rot_cols(w_ka, H_A, D_HEAD_A), w_va, w_za,
            w_qi, _rot_cols(w_qi, H_IDX, D_IDX),
            w_ki, w_kr, w_wi, z(28), ki_r, kr_r, z(32),
            w_kr, w_kr, w_kr, w_kr, kr_r, kr_r, kr_r, kr_r,
            w_ki, w_ki, ki_r, ki_r,
            w_ql, w_ckv, w_zb, w_ga, w_gb]
    out = jnp.concatenate(cols, axis=1)
    assert out.shape[1] == N_EXT
    return out.astype(BF16)


def _prep_w_uq(w_uq):
    k = w_uq.shape[0]
    w3 = w_uq.reshape(k, H_B, D_NOPE + D_ROPE)
    nope = w3[:, :, :D_NOPE].reshape(k, H_B * D_NOPE)
    pe = w3[:, :, D_NOPE:].reshape(k, H_B * D_ROPE)
    return jnp.concatenate([nope, pe, _rot_cols(pe, H_B, D_ROPE)], axis=1).astype(BF16)


def _rope_table(pos):
    def cs(d):
        inv = ROPE_THETA ** (-jnp.arange(0, d, 2, dtype=F32) / d)
        ang = pos.astype(F32)[:, None] * inv[None, :]
        c, s = jnp.cos(ang), jnp.sin(ang)
        return jnp.concatenate([c, c], axis=-1), jnp.concatenate([s, s], axis=-1)
    c64, s64 = cs(D_IDX)
    c32, s32 = cs(D_ROPE)
    t = pos.shape[0]
    wi_scale = jnp.full((t, H_IDX), H_IDX ** -0.5, F32)
    small_c = jnp.concatenate([c64, c32, wi_scale, jnp.zeros((t, 28), F32)], axis=-1)
    small_s = jnp.concatenate([s64, s32, jnp.zeros((t, 32), F32)], axis=-1)
    return jnp.concatenate([jnp.tile(c64, (1, 8)), jnp.tile(s64, (1, 8)), small_c, small_s,
                            jnp.tile(c32, (1, 8)), jnp.tile(s32, (1, 8))], axis=-1)


def _pages_per_step(n_pages, want):
    g = min(want, n_pages)
    while n_pages % g:
        g -= 1
    return g


def _token_tile(t):
    for tm in (256, 128, 64, 32, 16, 8):
        if t % tm == 0:
            return tm
    raise ValueError(f"token count {t} is not a multiple of 8")


def kernel(x_prompt, x_sample, cache_k_a, cache_v_a, cache_idx_k, cache_lat, cache_kpe, page_table,
           c_prompt, c_sample, w_ada, b_ada, g_norm, w_in, g_q_lora, w_uq, g_kv, w_ukv,
           w_o_a, w_o_b, w_out, g_final):
    b, t, _ = x_prompt.shape
    db, ts, _ = x_sample.shape
    n_pages = page_table.shape[1]
    n_past = n_pages * PAGE_SIZE
    n_pool = cache_k_a.shape[0]
    assert ts <= ROWS_S and t % KEY_CHUNK == 0

    win = _prep_w_in(w_in)
    wuq = _prep_w_uq(w_uq)
    wkt = w_ukv[:, :, :D_NOPE].reshape(D_C, H_B * D_NOPE).T.astype(BF16)
    wv = w_ukv[:, :, D_NOPE:]
    wvp = jnp.zeros((H_B, D_C, LANES), F32)
    for hh in range(H_B):
        wvp = wvp.at[hh, :, (hh % 2) * D_V:(hh % 2 + 1) * D_V].set(wv[:, hh, :])
    wvp = wvp.astype(BF16)
    gn = g_norm.reshape(1, -1)
    gq = g_q_lora.reshape(1, -1)
    gkv = g_kv.reshape(1, -1)
    gf = g_final.reshape(1, -1)
    woa, wob, wout = w_o_a.astype(BF16), w_o_b.astype(BF16), w_out.astype(BF16)

    pad_c = (-(b + db)) % 8
    c_all = jnp.concatenate([c_prompt, c_sample, jnp.zeros((pad_c, D_MODEL), F32)], axis=0)
    mod = _mod(c_all, w_ada, b_ada)
    mod_p = mod[:b].reshape(b, 1, 3 * D_MODEL)
    mod_s = jnp.repeat(mod[b:b + db], ts, axis=0).reshape(1, db * ts, 3 * D_MODEL)

    tab_p = _rope_table(jnp.arange(t))
    fp = _front(x_prompt, mod_p, tab_p, gn, gq, gkv, win, wuq, _token_tile(t), False)
    (qa_p, ka_p, kab_p, va_p, vab_p, za_p, qi_p, sm_p, krr_p, kir_p,
     qn_p, qp_p, lat_p, latb_p, zb_p, ga_p, gb_p) = fp
    oa_p = _dsa_p(qi_p, sm_p, kir_p, qa_p, kab_p, vab_p, min(TOPK_MAX, t // 4))
    ob_p = _mla_p(qn_p, qp_p, latb_p, krr_p, wkt, wvp)
    y_prompt = _back(x_prompt, mod_p, oa_p, za_p, ob_p, zb_p, ga_p, gb_p, woa, wob, wout, gf,
                     _token_tile(t), False)

    ns = db * ts
    tab_s = jnp.tile(_rope_table(n_past + jnp.arange(ts)), (db, 1))
    xs = x_sample.reshape(1, ns, D_MODEL)
    fs = _front(xs, mod_s, tab_s, gn, gq, gkv, win, wuq, _token_tile(ns), True)
    (qa_s, ka_s, kab_s, va_s, vab_s, za_s, qi_s, sm_s, krr_s, kir_s,
     qn_s, qp_s, lat_s, latb_s, zb_s, ga_s, gb_s) = [a[0] for a in fs]

    def pad_rows(a, r):
        return jnp.pad(a, ((0, 0), (0, r - a.shape[1]), (0, 0)))

    k_sel_s = min(TOPK_MAX, (n_past + ts) // 4)
    q_stack = pad_rows(qi_s.reshape(db, ts, H_IDX, D_IDX).transpose(0, 2, 1, 3).reshape(db * H_IDX, ts, D_IDX),
                       ROWS_S).reshape(db, H_IDX * ROWS_S, D_IDX)
    w_i = sm_s[:, 96:96 + H_IDX].reshape(db, ts, H_IDX).transpose(0, 2, 1) * (D_IDX ** -0.5)
    w_stack = jnp.broadcast_to(pad_rows(w_i.reshape(db * H_IDX, ts, 1), ROWS_S), (db * H_IDX, ROWS_S, LANES))
    w_stack = w_stack.reshape(db, H_IDX * ROWS_S, LANES)
    ki_new = pad_rows(sm_s[:, :D_IDX].reshape(db, ts, D_IDX), KEY_CHUNK)
    gp = _pages_per_step(n_pages, 16)
    mask = _sel_s(page_table, q_stack, w_stack, ki_new, cache_idx_k, ts, k_sel_s, gp)

    qa_pad = pad_rows(qa_s.reshape(db, ts, D_A), ROWS_S)
    ka_new = pad_rows(ka_s.reshape(db, ts, D_A), KEY_CHUNK)
    va_new = pad_rows(va_s.reshape(db, ts, D_A), KEY_CHUNK)
    oa_s = _dsa_s(page_table, qa_pad, mask, ka_new, va_new,
                  cache_k_a.reshape(n_pool, PAGE_SIZE, D_A), cache_v_a.reshape(n_pool, PAGE_SIZE, D_A),
                  _pages_per_step(n_pages, 8))[:, :ts].reshape(1, ns, D_A)

    qn_pad = pad_rows(qn_s.reshape(db, ts, D_A), ROWS_S)
    qpe_rows = pad_rows(qp_s.reshape(db, ts, H_B, D_ROPE).transpose(0, 2, 1, 3).reshape(db * H_B, ts, D_ROPE),
                        ROWS_S).reshape(db, H_B * ROWS_S, D_ROPE)
    lat_new = pad_rows(lat_s.reshape(db, ts, D_C), KEY_CHUNK)
    kpe_new = pad_rows(sm_s[:, D_IDX:D_IDX + D_ROPE].reshape(db, ts, D_ROPE), KEY_CHUNK)
    ob_s = _mla_s(page_table, qn_pad, qpe_rows, lat_new, kpe_new, wkt, wvp, cache_lat, cache_kpe, ts,
                  _pages_per_step(n_pages, 8))[:, :ts].reshape(1, ns, D_B)

    y_sample = _back(xs, mod_s, oa_s, za_s[None], ob_s, zb_s[None], ga_s[None], gb_s[None],
                     woa, wob, wout, gf, _token_tile(ns), True).reshape(db, ts, D_MODEL)

    return (y_prompt, y_sample,
            ka_p.reshape(b, t, H_A, D_HEAD_A), va_p.reshape(b, t, H_A, D_HEAD_A),
            sm_p[:, :, :D_IDX], lat_p, sm_p[:, :, D_IDX:D_IDX + D_ROPE],
            ka_s.reshape(db, ts, H_A, D_HEAD_A), va_s.reshape(db, ts, H_A, D_HEAD_A),
            sm_s[:, :D_IDX].reshape(db, ts, D_IDX), lat_s.reshape(db, ts, D_C),
            sm_s[:, D_IDX:D_IDX + D_ROPE].reshape(db, ts, D_ROPE))
```

```python
import functools

import jax
import jax.numpy as jnp
import numpy as np
from jax import lax
from jax.experimental import pallas as pl
from jax.experimental.pallas import tpu as pltpu

F32 = jnp.float32
BF16 = jnp.bfloat16
I32 = jnp.int32

D_MODEL = 1024
PAGE_SIZE = 128
H_A = 8
D_HEAD_A = 64
D_A = H_A * D_HEAD_A
H_IDX = 4
D_IDX = 64
TOPK_MAX = 256
H_B = 8
D_NOPE = 64
D_ROPE = 32
D_V = 64
D_Q_LORA = 384
D_C = 256
D_B = H_B * D_V
MLA_SCALE = (D_NOPE + D_ROPE) ** -0.5
ROPE_THETA = 10000.0
NORM_EPS = 1e-6
SPLIT_SIZES = (D_A, D_A, D_A, D_A, H_IDX * D_IDX, D_IDX, H_IDX, D_Q_LORA, D_C, D_ROPE, D_B, D_MODEL, D_MODEL)

LANES = 128
SUBLANES = 8
TQ = 128
KB = 256
INT_MIN = -(2 ** 31)
NEG_BIG = -1e30
VMEM_LIMIT = 56 * 1024 * 1024
ROWS_S = SUBLANES
PAGES_PER_STEP = 16
SEL_BATCHES = 8

C_QA, C_KA, C_QAR, C_KAR, C_VA, C_ZA = 0, 512, 1024, 1536, 2048, 2560
C_KR, C_KRR = 3072, 3200
C_QL, C_CKV, C_ZB, C_GA, C_GB = 3328, 3712, 3968, 4480, 5504
N_MAIN = 6528
I_QI, I_QIR, I_SM, I_SMR, I_KI, I_KIR, N_IDX = 0, 256, 512, 640, 768, 896, 1024
T_CA, T_SA, T_CS, T_SS, T_CP, T_SP, T_W = 0, 512, 1024, 1152, 1280, 1536, 1792


def _cparams(sem):
    return pltpu.CompilerParams(dimension_semantics=sem, vmem_limit_bytes=VMEM_LIMIT)


def _sigmoid(x):
    return 1.0 / (1.0 + jnp.exp(-x))


def _dot(a, b, precision=None):
    return jnp.dot(a, b, preferred_element_type=F32, precision=precision)


def _dot_nt(a, b, precision=None):
    return lax.dot_general(a, b, (((1,), (1,)), ((), ())), preferred_element_type=F32, precision=precision)


HI = lax.Precision.HIGHEST


def _sortable(x):
    bits = lax.bitcast_convert_type(x + 0.0, I32)
    return bits ^ ((bits >> 31) & 0x7FFFFFFF)


def _rms(x, g):
    return x * lax.rsqrt(jnp.mean(x * x, axis=-1, keepdims=True) + NORM_EPS) * g


def _mod_kernel(c_ref, w_ref, b_ref, o_ref):
    c = c_ref[...]
    o_ref[...] = _dot(c * _sigmoid(c), w_ref[...], precision=HI) + b_ref[...]


def _mod(c_all, w_ada, b_ada):
    r = c_all.shape[0]
    return pl.pallas_call(
        _mod_kernel,
        grid=(3,),
        in_specs=[pl.BlockSpec((r, D_MODEL), lambda j: (0, 0)),
                  pl.BlockSpec((D_MODEL, D_MODEL), lambda j: (0, j)),
                  pl.BlockSpec((1, D_MODEL), lambda j: (0, j))],
        out_specs=pl.BlockSpec((r, D_MODEL), lambda j: (0, j)),
        out_shape=jax.ShapeDtypeStruct((r, 3 * D_MODEL), F32),
        compiler_params=_cparams(("arbitrary",)),
        name="mod",
    )(c_all, w_ada, b_ada.reshape(1, -1))


def _modulated(x_ref, mod_ref, gn_ref):
    mod = mod_ref[...]
    return _rms(x_ref[...], gn_ref[...]) * (1.0 + mod[:, D_MODEL:2 * D_MODEL]) + mod[:, :D_MODEL]


def _front_kernel(x_ref, mod_ref, tab_ref, gn_ref, gq_ref, gkv_ref, win_ref, wuq_ref,
                  qa_ref, ka_ref, kab_ref, va_ref, vt_ref, za_ref, krr_ref,
                  qn_ref, qp_ref, lat_ref, latb_ref, zb_ref, ga_ref, gb_ref):
    h = _modulated(x_ref, mod_ref, gn_ref).astype(BF16)

    def proj(lo, hi):
        return _dot(h, win_ref[:, lo:hi])

    cos_a = tab_ref[:, T_CA:T_CA + 512]
    sin_a = tab_ref[:, T_SA:T_SA + 512]
    cos_p = tab_ref[:, T_CP:T_CP + 256]
    sin_p = tab_ref[:, T_SP:T_SP + 256]
    a = proj(C_QA, C_VA)
    qa_ref[...] = (a[:, 0:512] * cos_a + a[:, 1024:1536] * sin_a).astype(BF16)
    ka = a[:, 512:1024] * cos_a + a[:, 1536:2048] * sin_a
    ka_ref[...] = ka
    kab_ref[...] = ka.astype(BF16)
    a = proj(C_VA, C_QL)
    va = a[:, 0:512]
    va_ref[...] = va
    vt_ref[...] = va.T.astype(BF16)
    za_ref[...] = a[:, 512:1024]
    krr_ref[...] = (a[:, 1024:1152] * cos_p[:, 0:128] + a[:, 1152:1280] * sin_p[:, 0:128]).astype(BF16)
    a = proj(C_QL, C_ZB)
    qln = _rms(a[:, 0:D_Q_LORA], gq_ref[...])
    qb = _dot(qln.astype(BF16), wuq_ref[...])
    qn_ref[...] = qb[:, 0:512].astype(BF16)
    qp_ref[...] = (qb[:, 512:768] * cos_p + qb[:, 768:1024] * sin_p).astype(BF16)
    lat = _rms(a[:, D_Q_LORA:D_Q_LORA + D_C], gkv_ref[...])
    lat_ref[...] = lat
    latb_ref[...] = lat.astype(BF16)
    a = proj(C_ZB, C_GB)
    zb_ref[...] = a[:, 0:512]
    ga_ref[...] = a[:, 512:1536]
    gb_ref[...] = proj(C_GB, N_MAIN)


_FRONT_OUT = (
    (512, BF16), (512, F32), (512, BF16), (512, F32), None, (512, F32), (128, BF16),
    (512, BF16), (256, BF16), (256, F32), (256, BF16), (512, F32), (1024, F32), (1024, F32))


def _mod_spec(tm, per_row_mod):
    if per_row_mod:
        return pl.BlockSpec((None, tm, 3 * D_MODEL), lambda b, i: (b, i, 0))
    return pl.BlockSpec((None, 1, 3 * D_MODEL), lambda b, i: (b, 0, 0))


def _front(x, mod, tab, gn, gq, gkv, win, wuq, per_row_mod):
    g, t, _ = x.shape
    tm = KB
    nt = t // tm
    const = lambda b, i: (0, 0)
    in_specs = [
        pl.BlockSpec((None, tm, D_MODEL), lambda b, i: (b, i, 0)),
        _mod_spec(tm, per_row_mod),
        pl.BlockSpec((tm, T_W), lambda b, i: (i, 0)),
        pl.BlockSpec((1, D_MODEL), const),
        pl.BlockSpec((1, D_Q_LORA), const),
        pl.BlockSpec((1, D_C), const),
        pl.BlockSpec((D_MODEL, N_MAIN), const),
        pl.BlockSpec((D_Q_LORA, 1024), const),
    ]
    out_specs, out_shape = [], []
    for item in _FRONT_OUT:
        if item is None:
            out_specs.append(pl.BlockSpec((None, None, D_A, tm), lambda b, i: (b, i, 0, 0)))
            out_shape.append(jax.ShapeDtypeStruct((g, nt, D_A, tm), BF16))
        else:
            w, dt = item
            out_specs.append(pl.BlockSpec((None, tm, w), lambda b, i: (b, i, 0)))
            out_shape.append(jax.ShapeDtypeStruct((g, t, w), dt))
    return pl.pallas_call(
        _front_kernel, grid=(g, nt), in_specs=in_specs, out_specs=out_specs, out_shape=out_shape,
        compiler_params=_cparams(("arbitrary", "arbitrary")), name="front",
    )(x, mod, tab, gn, gq, gkv, win, wuq)


def _front_idx_kernel(x_ref, mod_ref, tab_ref, gn_ref, widx_ref, qi_ref, sm_ref, kir_ref):
    h = _modulated(x_ref, mod_ref, gn_ref)
    a = _dot(h, widx_ref[...], precision=HI)
    cos_a = tab_ref[:, T_CA:T_CA + 256]
    sin_a = tab_ref[:, T_SA:T_SA + 256]
    qi_ref[...] = a[:, I_QI:I_QI + 256] * cos_a + a[:, I_QIR:I_QIR + 256] * sin_a
    sm_ref[...] = (a[:, I_SM:I_SM + 128] * tab_ref[:, T_CS:T_CS + 128]
                   + a[:, I_SMR:I_SMR + 128] * tab_ref[:, T_SS:T_SS + 128])
    kir_ref[...] = a[:, I_KI:I_KI + 128] * cos_a[:, 0:128] + a[:, I_KIR:I_KIR + 128] * sin_a[:, 0:128]


def _front_idx(x, mod, tab, gn, widx, per_row_mod):
    g, t, _ = x.shape
    tm = KB
    nt = t // tm
    const = lambda b, i: (0, 0)
    widths = (256, 128, 128)
    return pl.pallas_call(
        _front_idx_kernel, grid=(g, nt),
        in_specs=[pl.BlockSpec((None, tm, D_MODEL), lambda b, i: (b, i, 0)),
                  _mod_spec(tm, per_row_mod),
                  pl.BlockSpec((tm, T_W), lambda b, i: (i, 0)),
                  pl.BlockSpec((1, D_MODEL), const),
                  pl.BlockSpec((D_MODEL, N_IDX), const)],
        out_specs=[pl.BlockSpec((None, tm, w), lambda b, i: (b, i, 0)) for w in widths],
        out_shape=[jax.ShapeDtypeStruct((g, t, w), F32) for w in widths],
        compiler_params=_cparams(("arbitrary", "arbitrary")), name="front_idx",
    )(x, mod, tab, gn, widx)


def _select(count_fn, t_shape, k_sel, idx_bits):
    kf = float(k_sel)

    def bit_step(it, t):
        cand = t + lax.shift_left(jnp.int32(1), 31 - it)
        return jnp.where(count_fn(lambda key, idx: key >= cand) >= kf, cand, t)

    t = lax.fori_loop(0, 32, bit_step, jnp.full(t_shape, INT_MIN, I32))
    need = kf - count_fn(lambda key, idx: key > t)
    n_tie = count_fn(lambda key, idx: key == t)

    def tie_step(it, p):
        cand = p + lax.shift_left(jnp.int32(1), idx_bits - 1 - it)
        return jnp.where(count_fn(lambda key, idx: (key == t) & (idx < cand)) < need, cand, p)

    def search():
        return lax.fori_loop(0, idx_bits, tie_step, jnp.zeros(t_shape, I32))

    all_taken = jnp.where((n_tie <= need) | (t == INT_MIN), 1.0, 0.0)
    p = lax.cond(jnp.min(all_taken) > 0.5, lambda: jnp.full(t_shape, 2 ** idx_bits - 1, I32), search)
    return t, p


def _selected(key, idx, t, p):
    return ((key > t) | ((key == t) & (idx <= p))) & (key != INT_MIN)


def _dsa_p_kernel(qi_ref, sm_ref, kir_ref, qa_ref, kab_ref, vt_ref, oa_ref,
                  keys_ref, st_ref, qis_ref, qs_ref, acc_ref, *, k_sel, idx_bits):
    i = pl.program_id(1)
    nblk = ((i + 1) * TQ + KB - 1) // KB
    lane_q = lax.broadcasted_iota(I32, (TQ, LANES), 1)
    lo_q = lane_q < 64
    kpos0 = lax.broadcasted_iota(I32, (KB, TQ), 0)
    qpos = i * TQ + lax.broadcasted_iota(I32, (KB, TQ), 1)

    qi = qi_ref[...]
    for j in range(2):
        col = qi[:, j * LANES:(j + 1) * LANES]
        qis_ref[(2 * j) * TQ:(2 * j + 1) * TQ, :] = jnp.where(lo_q, col, 0.0)
        qis_ref[(2 * j + 1) * TQ:(2 * j + 2) * TQ, :] = jnp.where(lo_q, 0.0, col)
    sm_t = sm_ref[...].T
    w_heads = [sm_t[96 + hh:97 + hh, :] * (D_IDX ** -0.5) for hh in range(H_IDX)]

    def score_block(blk, carry):
        kc = kir_ref[pl.ds(pl.multiple_of(blk * KB, KB), KB), :]
        s = _dot_nt(kc, qis_ref[...], precision=HI)
        tot = jnp.zeros((KB, TQ), F32)
        for hh in range(H_IDX):
            tot = tot + jnp.maximum(s[:, hh * TQ:(hh + 1) * TQ], 0.0) * w_heads[hh]
        keys_ref[blk] = jnp.where((blk * KB + kpos0) <= qpos, _sortable(tot), INT_MIN)
        return carry

    lax.fori_loop(0, nblk, score_block, 0)

    def count_fn(pred):
        def body(blk, acc):
            x = jnp.where(pred(keys_ref[blk], blk * KB + kpos0), 1.0, 0.0)
            rows = KB
            while rows > SUBLANES:
                rows //= 2
                x = x[:rows] + x[rows:]
            return acc + x
        acc = lax.fori_loop(0, nblk, body, jnp.zeros((SUBLANES, TQ), F32))
        return jnp.sum(acc, axis=0, keepdims=True)

    t, p = _select(count_fn, (1, TQ), k_sel, idx_bits)

    qa = qa_ref[...] * (D_HEAD_A ** -0.5)
    zero = jnp.zeros((TQ, LANES), BF16)
    for j in range(4):
        col = qa[:, j * LANES:(j + 1) * LANES]
        qs_ref[j] = jnp.concatenate([jnp.where(lo_q, col, zero), jnp.where(lo_q, zero, col)], axis=0)

    def phase1(blk, mx):
        off = pl.multiple_of(blk * KB, KB)
        sel = _selected(keys_ref[blk], blk * KB + kpos0, t, p)
        bias = jnp.where(sel, 0.0, NEG_BIG)
        bias2 = jnp.concatenate([bias, bias], axis=1)
        out = []
        for j in range(4):
            s = _dot_nt(kab_ref[pl.ds(off, KB), j * LANES:(j + 1) * LANES], qs_ref[j]) + bias2
            st_ref[j, blk] = s
            out.append(jnp.maximum(mx[j], jnp.max(s, axis=0, keepdims=True)))
        return tuple(out)

    m = lax.fori_loop(0, nblk, phase1, tuple(jnp.full((1, 2 * TQ), NEG_BIG, F32) for _ in range(4)))
    acc_ref[...] = jnp.zeros_like(acc_ref)

    def phase2(blk, ls):
        out = []
        for j in range(4):
            pr = jnp.exp(st_ref[j, blk] - m[j])
            out.append(ls[j] + jnp.sum(pr, axis=0, keepdims=True))
            acc_ref[j] += _dot(vt_ref[blk, j * LANES:(j + 1) * LANES, :], pr.astype(BF16))
        return tuple(out)

    ls = lax.fori_loop(0, nblk, phase2, tuple(jnp.zeros((1, 2 * TQ), F32) for _ in range(4)))
    feat_lo = lax.broadcasted_iota(I32, (LANES, TQ), 0) < 64
    for j in range(4):
        o = acc_ref[j] * (1.0 / ls[j])
        oa_ref[:, j * LANES:(j + 1) * LANES] = jnp.where(feat_lo, o[:, :TQ], o[:, TQ:]).T


def _dsa_p(qi, sm, kir, qa, kab, vt, k_sel):
    b, t, _ = qa.shape
    nq = t // TQ
    nkb = t // KB
    idx_bits = max(1, int(np.ceil(np.log2(t))))
    blk = lambda w: pl.BlockSpec((None, TQ, w), lambda bb, i: (bb, i, 0))
    full = lambda w: pl.BlockSpec((None, t, w), lambda bb, i: (bb, 0, 0))
    kern = functools.partial(_dsa_p_kernel, k_sel=k_sel, idx_bits=idx_bits)
    return pl.pallas_call(
        kern, grid=(b, nq),
        in_specs=[blk(256), blk(128), full(128), blk(512), full(512),
                  pl.BlockSpec((None, nkb, D_A, KB), lambda bb, i: (bb, 0, 0, 0))],
        out_specs=blk(512),
        out_shape=jax.ShapeDtypeStruct((b, t, D_A), F32),
        scratch_shapes=[pltpu.VMEM((nkb, KB, TQ), I32),
                        pltpu.VMEM((4, nkb, KB, 2 * TQ), F32),
                        pltpu.VMEM((H_IDX * TQ, LANES), F32),
                        pltpu.VMEM((4, 2 * TQ, LANES), BF16),
                        pltpu.VMEM((4, LANES, 2 * TQ), F32)],
        compiler_params=_cparams(("arbitrary", "arbitrary")), name="dsa_p",
    )(qi, sm, kir, qa, kab, vt)


def _mla_p_kernel(qn_ref, qp_ref, latb_ref, krr_ref, wkt_ref, wvp_ref, ob_ref,
                  qabs_ref, qpe_ref, s_ref, acc_ref, mx_ref, l_ref):
    i = pl.program_id(1)
    nblk = ((i + 1) * TQ + KB - 1) // KB
    rows = H_B * TQ
    lane = lax.broadcasted_iota(I32, (TQ, LANES), 1)
    zero = jnp.zeros((TQ, LANES), BF16)
    qn = qn_ref[...]
    qp = qp_ref[...]
    for hh in range(H_B):
        j, half = hh // 2, hh % 2
        masked = jnp.where((lane // 64) == half, qn[:, j * LANES:(j + 1) * LANES], zero)
        qabs_ref[hh * TQ:(hh + 1) * TQ, :] = _dot(masked, wkt_ref[j * LANES:(j + 1) * LANES, :]).astype(BF16)
        j4, quarter = hh // 4, hh % 4
        qpe_ref[hh * TQ:(hh + 1) * TQ, :] = jnp.where((lane // 32) == quarter, qp[:, j4 * LANES:(j4 + 1) * LANES], zero)
    mx_ref[...] = jnp.full_like(mx_ref, NEG_BIG)

    def scores(blk):
        off = pl.multiple_of(blk * KB, KB)
        return (_dot_nt(qabs_ref[...], latb_ref[pl.ds(off, KB), :])
                + _dot_nt(qpe_ref[...], krr_ref[pl.ds(off, KB), :])) * MLA_SCALE

    def keep(blk, s):
        s_ref[blk] = s
        mx_ref[...] = jnp.maximum(mx_ref[...], jnp.maximum(s[:, :LANES], s[:, LANES:]))

    def phase1(blk, carry):
        keep(blk, scores(blk))
        return carry

    lax.fori_loop(0, nblk - 1, phase1, 0)
    last = nblk - 1
    qpos = i * TQ + (lax.broadcasted_iota(I32, (rows, KB), 0) & (TQ - 1))
    kpos = last * KB + lax.broadcasted_iota(I32, (rows, KB), 1)
    keep(last, jnp.where(kpos <= qpos, scores(last), NEG_BIG))

    mx_ref[...] = jnp.broadcast_to(jnp.max(mx_ref[...], axis=-1, keepdims=True), (rows, LANES))
    acc_ref[...] = jnp.zeros_like(acc_ref)
    l_ref[...] = jnp.zeros_like(l_ref)

    def phase2(blk, carry):
        off = pl.multiple_of(blk * KB, KB)
        s = s_ref[blk]
        p0 = jnp.exp(s[:, :LANES] - mx_ref[...])
        p1 = jnp.exp(s[:, LANES:] - mx_ref[...])
        l_ref[...] += p0 + p1
        pr = jnp.concatenate([p0, p1], axis=1).astype(BF16)
        acc_ref[...] += _dot(pr, latb_ref[pl.ds(off, KB), :])
        return carry

    lax.fori_loop(0, nblk, phase2, 0)
    inv = 1.0 / jnp.sum(l_ref[...], axis=-1, keepdims=True)
    o_lat = (acc_ref[...] * inv).astype(BF16)
    for j in range(4):
        ob_ref[:, j * LANES:(j + 1) * LANES] = (
            _dot(o_lat[(2 * j) * TQ:(2 * j + 1) * TQ], wvp_ref[2 * j])
            + _dot(o_lat[(2 * j + 1) * TQ:(2 * j + 2) * TQ], wvp_ref[2 * j + 1]))


def _mla_p(qn, qp, latb, krr, wkt, wvp):
    b, t, _ = qn.shape
    nq = t // TQ
    nkb = t // KB
    rows = H_B * TQ
    blk = lambda w: pl.BlockSpec((None, TQ, w), lambda bb, i: (bb, i, 0))
    full = lambda w: pl.BlockSpec((None, t, w), lambda bb, i: (bb, 0, 0))
    return pl.pallas_call(
        _mla_p_kernel, grid=(b, nq),
        in_specs=[blk(512), blk(256), full(256), full(128),
                  pl.BlockSpec((D_A, D_C), lambda bb, i: (0, 0)),
                  pl.BlockSpec((H_B, D_C, LANES), lambda bb, i: (0, 0, 0))],
        out_specs=blk(512),
        out_shape=jax.ShapeDtypeStruct((b, t, D_B), F32),
        scratch_shapes=[pltpu.VMEM((rows, D_C), BF16),
                        pltpu.VMEM((rows, LANES), BF16),
                        pltpu.VMEM((nkb, rows, KB), F32),
                        pltpu.VMEM((rows, D_C), F32),
                        pltpu.VMEM((rows, LANES), F32),
                        pltpu.VMEM((rows, LANES), F32)],
        compiler_params=_cparams(("arbitrary", "arbitrary")), name="mla_p",
    )(qn, qp, latb, krr, wkt, wvp)


class _PagePipe:
    def __init__(self, pt_ref, hbm_refs, buf_refs, sem_ref, gpages):
        self.pt, self.hbm, self.buf, self.sem, self.g = pt_ref, hbm_refs, buf_refs, sem_ref, gpages
        bb, s = pl.program_id(0), pl.program_id(1)
        db, nsteps = pl.num_programs(0), pl.num_programs(1)
        n = bb * nsteps + s
        self.first = n == 0
        self.final = n == db * nsteps - 1
        self.slot = lax.rem(n, 2)
        wrap = s == nsteps - 1
        nb = jnp.minimum(jnp.where(wrap, bb + 1, bb), db - 1)
        ns = jnp.where(wrap, 0, s + 1)
        self.cur = self._copies(bb, s, self.slot)
        self.nxt = self._copies(nb, ns, 1 - self.slot)

    def _copies(self, bb, s, slot):
        out = []
        for a, (hbm, buf) in enumerate(zip(self.hbm, self.buf)):
            out.append([pltpu.make_async_copy(hbm.at[self.pt[s * self.g + g, bb]], buf.at[slot, g],
                                              self.sem.at[slot, a]) for g in range(self.g)])
        return out

    def wait_current(self):
        @pl.when(self.first)
        def _():
            for per_array in self.cur:
                for c in per_array:
                    c.start()
        for per_array in self.cur:
            for c in per_array:
                c.wait()

    def page(self, a, g):
        return self.buf[a][self.slot, g]

    def prefetch(self, g):
        for per_array in self.nxt:
            per_array[g].start()

    def drain(self):
        @pl.when(self.final)
        def _():
            for per_array in self.nxt:
                for c in per_array:
                    c.wait()


def _page_scratch(shapes, gpages):
    return ([pltpu.VMEM((2, gpages) + shp, F32) for shp in shapes]
            + [pltpu.SemaphoreType.DMA((2, len(shapes)))])


_ANY = pl.BlockSpec(memory_space=pl.ANY)


def _sc_s_kernel(pt_ref, q_ref, w_ref, knew_ref, cache_ref, keys_ref, kbuf_ref, sem_ref, *, n_tok, gpages, n_pages):
    s_idx = pl.program_id(1)
    pipe = _PagePipe(pt_ref, [cache_ref], [kbuf_ref], sem_ref, gpages)
    lane = lax.broadcasted_iota(I32, (ROWS_S, LANES), 1)
    row = lax.broadcasted_iota(I32, (ROWS_S, LANES), 0)
    q = q_ref[...]
    w = w_ref[...]

    def scores(kt):
        s = _dot(q, kt, precision=HI)
        out = []
        for g in range(kt.shape[1] // LANES):
            tot = jnp.zeros((ROWS_S, LANES), F32)
            for hh in range(H_IDX):
                tot = tot + (jnp.maximum(s[hh * ROWS_S:(hh + 1) * ROWS_S, g * LANES:(g + 1) * LANES], 0.0)
                             * w[hh * ROWS_S:(hh + 1) * ROWS_S])
            out.append(_sortable(tot))
        return out

    pipe.wait_current()
    pages = []
    for g in range(gpages):
        pages.append(pipe.page(0, g))
        pipe.prefetch(g)
    for g, key in enumerate(scores(jnp.concatenate(pages, axis=1))):
        keys_ref[s_idx * gpages + g] = jnp.where(row < n_tok, key, INT_MIN)

    @pl.when(s_idx == n_pages // gpages - 1)
    def _():
        keys_ref[n_pages] = jnp.where((lane <= row) & (row < n_tok), scores(knew_ref[...])[0], INT_MIN)

    pipe.drain()


def _sc_s(pt_t, q_stack, w_stack, kt_new, cache_idx_kt, n_tok, gpages):
    n_pages, db = pt_t.shape
    kern = functools.partial(_sc_s_kernel, n_tok=n_tok, gpages=gpages, n_pages=n_pages)
    per_b = lambda r, w: pl.BlockSpec((None, r, w), lambda bb, s, pt: (bb, 0, 0))
    grid_spec = pltpu.PrefetchScalarGridSpec(
        num_scalar_prefetch=1, grid=(db, n_pages // gpages),
        in_specs=[per_b(H_IDX * ROWS_S, D_IDX), per_b(H_IDX * ROWS_S, LANES), per_b(D_IDX, LANES), _ANY],
        out_specs=pl.BlockSpec((None, n_pages + 1, ROWS_S, LANES), lambda bb, s, pt: (bb, 0, 0, 0)),
        scratch_shapes=_page_scratch([(D_IDX, PAGE_SIZE)], gpages))
    return pl.pallas_call(
        kern, grid_spec=grid_spec,
        out_shape=jax.ShapeDtypeStruct((db, n_pages + 1, ROWS_S, LANES), I32),
        compiler_params=_cparams(("arbitrary", "arbitrary")), name="sc_s",
    )(pt_t, q_stack, w_stack, kt_new, cache_idx_kt)


def _sel_s_kernel(keys_ref, mask_ref, *, nb, nchunk, k_sel, idx_bits):
    rows = nb * ROWS_S
    lane = lax.broadcasted_iota(I32, (rows, LANES), 1)

    def chunk(c):
        return keys_ref[:, c].reshape(rows, LANES)

    def count_fn(pred):
        acc = jnp.zeros((rows, LANES), F32)
        for c in range(nchunk):
            acc = acc + jnp.where(pred(chunk(c), c * LANES + lane), 1.0, 0.0)
        return jnp.sum(acc, axis=-1, keepdims=True)

    t, p = _select(count_fn, (rows, LANES), k_sel, idx_bits)
    for c in range(nchunk):
        sel = _selected(chunk(c), c * LANES + lane, t, p)
        mask_ref[:, c] = jnp.where(sel, 1.0, 0.0).reshape(nb, ROWS_S, LANES)


def _sel_s(keys, k_sel):
    db, nchunk, _, _ = keys.shape
    nb = SEL_BATCHES if db % SEL_BATCHES == 0 else 1
    idx_bits = int(np.ceil(np.log2(nchunk * LANES)))
    kern = functools.partial(_sel_s_kernel, nb=nb, nchunk=nchunk, k_sel=k_sel, idx_bits=idx_bits)
    spec = pl.BlockSpec((nb, nchunk, ROWS_S, LANES), lambda g: (g, 0, 0, 0))
    return pl.pallas_call(
        kern, grid=(db // nb,), in_specs=[spec], out_specs=spec,
        out_shape=jax.ShapeDtypeStruct(keys.shape, F32),
        compiler_params=_cparams(("arbitrary",)), name="sel_s",
    )(keys)


def _dsa_s_kernel(pt_ref, qa_ref, mask_ref, knew_ref, vnew_ref, ck_ref, cv_ref, oa_ref,
                  qbd_ref, acc_ref, m_ref, l_ref, kbuf_ref, vbuf_ref, sem_ref, *, gpages, n_pages):
    s_idx = pl.program_id(1)
    pipe = _PagePipe(pt_ref, [ck_ref, cv_ref], [kbuf_ref, vbuf_ref], sem_ref, gpages)
    rows = H_A * ROWS_S
    lane = lax.broadcasted_iota(I32, (rows, D_A), 1)
    rowi = lax.broadcasted_iota(I32, (rows, D_A), 0)

    @pl.when(s_idx == 0)
    def _():
        qa = qa_ref[...] * (D_HEAD_A ** -0.5)
        rep = jnp.concatenate([qa] * H_A, axis=0)
        qbd_ref[...] = jnp.where((lane // D_HEAD_A) == (rowi // ROWS_S), rep, jnp.zeros_like(rep))
        acc_ref[...] = jnp.zeros_like(acc_ref)
        m_ref[...] = jnp.full_like(m_ref, NEG_BIG)
        l_ref[...] = jnp.zeros_like(l_ref)

    def process(n, kt_of, vt_of, mask_of, after_scores=None):
        ss = []
        for g in range(n):
            bias = jnp.where(mask_of(g) > 0.5, 0.0, NEG_BIG)
            ss.append(_dot(qbd_ref[...], kt_of(g).astype(BF16)) + jnp.concatenate([bias] * H_A, axis=0))
            if after_scores is not None:
                after_scores(g)
        s = jnp.concatenate(ss, axis=1)
        m_old = m_ref[...]
        m_new = jnp.maximum(m_old, jnp.max(s, axis=-1, keepdims=True))
        pr = jnp.exp(s - m_new)
        alpha = jnp.exp(m_old - m_new)
        l_ref[...] = alpha * l_ref[...] + jnp.sum(pr, axis=-1, keepdims=True)
        m_ref[...] = m_new
        pv = jnp.zeros((rows, D_A), F32)
        for g in range(n):
            pv = pv + _dot_nt(pr[:, g * LANES:(g + 1) * LANES].astype(BF16), vt_of(g).astype(BF16))
        acc_ref[...] = acc_ref[...] * alpha + pv

    pipe.wait_current()
    process(gpages, lambda g: pipe.page(0, g), lambda g: pipe.page(1, g),
            lambda g: mask_ref[s_idx * gpages + g], pipe.prefetch)

    @pl.when(s_idx == n_pages // gpages - 1)
    def _():
        process(1, lambda g: knew_ref[...], lambda g: vnew_ref[...], lambda g: mask_ref[n_pages])
        o = acc_ref[...] * (1.0 / l_ref[...])
        out = jnp.zeros((ROWS_S, D_A), F32)
        lane8 = lax.broadcasted_iota(I32, (ROWS_S, D_A), 1)
        for hh in range(H_A):
            out = out + jnp.where((lane8 // D_HEAD_A) == hh, o[hh * ROWS_S:(hh + 1) * ROWS_S], 0.0)
        oa_ref[...] = out

    pipe.drain()


def _dsa_s(pt_t, qa_pad, mask, kt_new, vt_new, cache_kt, cache_vt, gpages):
    n_pages, db = pt_t.shape
    rows = H_A * ROWS_S
    per_b = lambda r, w: pl.BlockSpec((None, r, w), lambda bb, s, pt: (bb, 0, 0))
    kern = functools.partial(_dsa_s_kernel, gpages=gpages, n_pages=n_pages)
    grid_spec = pltpu.PrefetchScalarGridSpec(
        num_scalar_prefetch=1, grid=(db, n_pages // gpages),
        in_specs=[per_b(ROWS_S, D_A),
                  pl.BlockSpec((None, n_pages + 1, ROWS_S, LANES), lambda bb, s, pt: (bb, 0, 0, 0)),
                  per_b(D_A, LANES), per_b(D_A, LANES), _ANY, _ANY],
        out_specs=per_b(ROWS_S, D_A),
        scratch_shapes=[pltpu.VMEM((rows, D_A), BF16),
                        pltpu.VMEM((rows, D_A), F32),
                        pltpu.VMEM((rows, 1), F32),
                        pltpu.VMEM((rows, 1), F32)]
                       + _page_scratch([(D_A, PAGE_SIZE), (D_A, PAGE_SIZE)], gpages))
    return pl.pallas_call(
        kern, grid_spec=grid_spec,
        out_shape=jax.ShapeDtypeStruct((db, ROWS_S, D_A), F32),
        compiler_params=_cparams(("arbitrary", "arbitrary")), name="dsa_s",
    )(pt_t, qa_pad, mask, kt_new, vt_new, cache_kt, cache_vt)


def _mla_s_kernel(pt_ref, qn_ref, qpe_ref, latnew_ref, kpenew_ref, wkt_ref, wvp_ref, cl_ref, cp_ref, ob_ref,
                  qabs_ref, acc_ref, m_ref, l_ref, lbuf_ref, pbuf_ref, sem_ref, *, gpages, n_pages, n_tok):
    s_idx = pl.program_id(1)
    pipe = _PagePipe(pt_ref, [cl_ref, cp_ref], [lbuf_ref, pbuf_ref], sem_ref, gpages)
    rows = H_B * ROWS_S

    @pl.when(s_idx == 0)
    def _():
        qn = qn_ref[...]
        rep = jnp.concatenate([qn] * H_B, axis=0)
        lane = lax.broadcasted_iota(I32, (rows, D_A), 1)
        r = lax.broadcasted_iota(I32, (rows, D_A), 0)
        blk = jnp.where((lane // D_NOPE) == (r // ROWS_S), rep, jnp.zeros_like(rep))
        qabs_ref[...] = _dot(blk, wkt_ref[...]).astype(BF16)
        acc_ref[...] = jnp.zeros_like(acc_ref)
        m_ref[...] = jnp.full_like(m_ref, NEG_BIG)
        l_ref[...] = jnp.zeros_like(l_ref)

    def process(n, lat_of, kpet_of, valid, after_scores=None):
        latbs, ss = [], []
        for g in range(n):
            lb = lat_of(g).astype(BF16)
            latbs.append(lb)
            ss.append((_dot_nt(qabs_ref[...], lb) + _dot(qpe_ref[...], kpet_of(g).astype(BF16))) * MLA_SCALE)
            if after_scores is not None:
                after_scores(g)
        s = jnp.concatenate(ss, axis=1)
        if valid is not None:
            s = jnp.where(valid, s, NEG_BIG)
        m_old = m_ref[...]
        m_new = jnp.maximum(m_old, jnp.max(s, axis=-1, keepdims=True))
        pr = jnp.exp(s - m_new)
        alpha = jnp.exp(m_old - m_new)
        l_ref[...] = alpha * l_ref[...] + jnp.sum(pr, axis=-1, keepdims=True)
        m_ref[...] = m_new
        pv = jnp.zeros((rows, D_C), F32)
        for g, lb in enumerate(latbs):
            pv = pv + _dot(pr[:, g * LANES:(g + 1) * LANES].astype(BF16), lb)
        acc_ref[...] = acc_ref[...] * alpha + pv

    pipe.wait_current()
    process(gpages, lambda g: pipe.page(0, g), lambda g: pipe.page(1, g), None, pipe.prefetch)

    @pl.when(s_idx == n_pages // gpages - 1)
    def _():
        kk = lax.broadcasted_iota(I32, (rows, LANES), 1)
        tok = lax.broadcasted_iota(I32, (rows, LANES), 0) & (ROWS_S - 1)
        process(1, lambda g: latnew_ref[...], lambda g: kpenew_ref[...], (kk <= tok) & (kk < n_tok))
        o_lat = (acc_ref[...] * (1.0 / l_ref[...])).astype(BF16)
        for j in range(4):
            ob_ref[:, j * LANES:(j + 1) * LANES] = (
                _dot(o_lat[(2 * j) * ROWS_S:(2 * j + 1) * ROWS_S], wvp_ref[2 * j])
                + _dot(o_lat[(2 * j + 1) * ROWS_S:(2 * j + 2) * ROWS_S], wvp_ref[2 * j + 1]))

    pipe.drain()


def _mla_s(pt_t, qn_pad, qpe_rows, lat_new, kpet_new, wkt, wvp, cache_lat, cache_kpet, n_tok, gpages):
    n_pages, db = pt_t.shape
    rows = H_B * ROWS_S
    per_b = lambda r, w: pl.BlockSpec((None, r, w), lambda bb, s, pt: (bb, 0, 0))
    kern = functools.partial(_mla_s_kernel, gpages=gpages, n_pages=n_pages, n_tok=n_tok)
    grid_spec = pltpu.PrefetchScalarGridSpec(
        num_scalar_prefetch=1, grid=(db, n_pages // gpages),
        in_specs=[per_b(ROWS_S, D_A), per_b(rows, D_ROPE), per_b(PAGE_SIZE, D_C), per_b(D_ROPE, LANES),
                  pl.BlockSpec((D_A, D_C), lambda bb, s, pt: (0, 0)),
                  pl.BlockSpec((H_B, D_C, LANES), lambda bb, s, pt: (0, 0, 0)), _ANY, _ANY],
        out_specs=per_b(ROWS_S, D_B),
        scratch_shapes=[pltpu.VMEM((rows, D_C), BF16),
                        pltpu.VMEM((rows, D_C), F32),
                        pltpu.VMEM((rows, 1), F32),
                        pltpu.VMEM((rows, 1), F32)]
                       + _page_scratch([(PAGE_SIZE, D_C), (D_ROPE, PAGE_SIZE)], gpages))
    return pl.pallas_call(
        kern, grid_spec=grid_spec,
        out_shape=jax.ShapeDtypeStruct((db, ROWS_S, D_B), F32),
        compiler_params=_cparams(("arbitrary", "arbitrary")), name="mla_s",
    )(pt_t, qn_pad, qpe_rows, lat_new, kpet_new, wkt, wvp, cache_lat, cache_kpet)


def _back_kernel(x_ref, mod_ref, oa_ref, za_ref, ob_ref, zb_ref, ga_ref, gb_ref,
                 woa_ref, wob_ref, wout_ref, gf_ref, y_ref):
    za = za_ref[...]
    zb = zb_ref[...]
    ya = _dot((oa_ref[...] * (za * _sigmoid(za))).astype(BF16), woa_ref[...])
    yb = _dot((ob_ref[...] * (zb * _sigmoid(zb))).astype(BF16), wob_ref[...])
    merged = _sigmoid(ga_ref[...]) * ya + _sigmoid(gb_ref[...]) * yb
    gate = mod_ref[...][:, 2 * D_MODEL:]
    h = x_ref[...] + gate * _dot(merged.astype(BF16), wout_ref[...])
    y_ref[...] = _rms(h, gf_ref[...])


def _back(x, mod, oa, za, ob, zb, ga, gb, woa, wob, wout, gf, per_row_mod):
    g, t, _ = x.shape
    tm = KB
    const = lambda b, i: (0, 0)
    blk = lambda w: pl.BlockSpec((None, tm, w), lambda b, i: (b, i, 0))
    return pl.pallas_call(
        _back_kernel, grid=(g, t // tm),
        in_specs=[blk(D_MODEL), _mod_spec(tm, per_row_mod), blk(512), blk(512), blk(512), blk(512),
                  blk(D_MODEL), blk(D_MODEL),
                  pl.BlockSpec((D_A, D_MODEL), const), pl.BlockSpec((D_B, D_MODEL), const),
                  pl.BlockSpec((D_MODEL, D_MODEL), const), pl.BlockSpec((1, D_MODEL), const)],
        out_specs=blk(D_MODEL),
        out_shape=jax.ShapeDtypeStruct((g, t, D_MODEL), F32),
        compiler_params=_cparams(("arbitrary", "arbitrary")), name="back",
    )(x, mod, oa, za, ob, zb, ga, gb, woa, wob, wout, gf)


def _rot_cols(w, n_heads, d):
    k = w.shape[0]
    w4 = w.reshape(k, n_heads, 2, d // 2)
    return jnp.concatenate([-w4[:, :, 1], w4[:, :, 0]], axis=-1).reshape(k, n_heads * d)


def _prep_w_in(w_in):
    offs = np.concatenate([[0], np.cumsum(SPLIT_SIZES)])
    seg = [w_in[:, offs[n]:offs[n + 1]] for n in range(len(SPLIT_SIZES))]
    w_qa, w_ka, w_va, w_za, w_qi, w_ki, w_wi, w_ql, w_ckv, w_kr, w_zb, w_ga, w_gb = seg
    k = w_in.shape[0]
    z = lambda n: jnp.zeros((k, n), w_in.dtype)
    ki_r = _rot_cols(w_ki, 1, D_IDX)
    kr_r = _rot_cols(w_kr, 1, D_ROPE)
    main = jnp.concatenate(
        [w_qa, w_ka, _rot_cols(w_qa, H_A, D_HEAD_A), _rot_cols(w_ka, H_A, D_HEAD_A), w_va, w_za,
         w_kr, w_kr, w_kr, w_kr, kr_r, kr_r, kr_r, kr_r,
         w_ql, w_ckv, w_zb, w_ga, w_gb], axis=1)
    idx = jnp.concatenate(
        [w_qi, _rot_cols(w_qi, H_IDX, D_IDX),
         w_ki, w_kr, w_wi, z(28), ki_r, kr_r, z(32),
         w_ki, w_ki, ki_r, ki_r], axis=1)
    assert main.shape[1] == N_MAIN and idx.shape[1] == N_IDX
    return main.astype(BF16), idx


def _prep_w_uq(w_uq):
    k = w_uq.shape[0]
    w3 = w_uq.reshape(k, H_B, D_NOPE + D_ROPE)
    nope = w3[:, :, :D_NOPE].reshape(k, H_B * D_NOPE)
    pe = w3[:, :, D_NOPE:].reshape(k, H_B * D_ROPE)
    return jnp.concatenate([nope, pe, _rot_cols(pe, H_B, D_ROPE)], axis=1).astype(BF16)


def _rope_table(pos):
    def cs(d):
        inv = ROPE_THETA ** (-jnp.arange(0, d, 2, dtype=F32) / d)
        ang = pos.astype(F32)[:, None] * inv[None, :]
        c, s = jnp.cos(ang), jnp.sin(ang)
        return jnp.concatenate([c, c], axis=-1), jnp.concatenate([s, s], axis=-1)
    c64, s64 = cs(D_IDX)
    c32, s32 = cs(D_ROPE)
    t = pos.shape[0]
    wi_scale = jnp.full((t, H_IDX), H_IDX ** -0.5, F32)
    small_c = jnp.concatenate([c64, c32, wi_scale, jnp.zeros((t, 28), F32)], axis=-1)
    small_s = jnp.concatenate([s64, s32, jnp.zeros((t, 32), F32)], axis=-1)
    return jnp.concatenate([jnp.tile(c64, (1, 8)), jnp.tile(s64, (1, 8)), small_c, small_s,
                            jnp.tile(c32, (1, 8)), jnp.tile(s32, (1, 8))], axis=-1)


def _pages_per_step(n_pages, want):
    g = min(want, n_pages)
    while n_pages % g:
        g -= 1
    return g


def kernel(x_prompt, x_sample, cache_k_a, cache_v_a, cache_idx_k, cache_lat, cache_kpe, page_table,
           c_prompt, c_sample, w_ada, b_ada, g_norm, w_in, g_q_lora, w_uq, g_kv, w_ukv,
           w_o_a, w_o_b, w_out, g_final):
    b, t, _ = x_prompt.shape
    db, ts, _ = x_sample.shape
    n_pages = page_table.shape[1]
    n_past = n_pages * PAGE_SIZE
    n_pool = cache_k_a.shape[0]
    ns = db * ts
    assert ts <= ROWS_S and t % KB == 0 and ns % KB == 0

    win, widx = _prep_w_in(w_in)
    wuq = _prep_w_uq(w_uq)
    wkt = w_ukv[:, :, :D_NOPE].reshape(D_C, H_B * D_NOPE).T.astype(BF16)
    wv = w_ukv[:, :, D_NOPE:]
    wvp = jnp.zeros((H_B, D_C, LANES), F32)
    for hh in range(H_B):
        wvp = wvp.at[hh, :, (hh % 2) * D_V:(hh % 2 + 1) * D_V].set(wv[:, hh, :])
    wvp = wvp.astype(BF16)
    gn = g_norm.reshape(1, -1)
    gq = g_q_lora.reshape(1, -1)
    gkv = g_kv.reshape(1, -1)
    gf = g_final.reshape(1, -1)
    woa, wob, wout = w_o_a.astype(BF16), w_o_b.astype(BF16), w_out.astype(BF16)

    pad_c = (-(b + db)) % SUBLANES
    c_all = jnp.concatenate([c_prompt, c_sample, jnp.zeros((pad_c, D_MODEL), F32)], axis=0)
    mod = _mod(c_all, w_ada, b_ada)
    mod_p = mod[:b].reshape(b, 1, 3 * D_MODEL)
    mod_s = jnp.repeat(mod[b:b + db], ts, axis=0).reshape(1, ns, 3 * D_MODEL)

    tab_p = _rope_table(jnp.arange(t))
    (qa_p, ka_p, kab_p, va_p, vt_p, za_p, krr_p, qn_p, qp_p, lat_p, latb_p, zb_p, ga_p, gb_p) = _front(
        x_prompt, mod_p, tab_p, gn, gq, gkv, win, wuq, False)
    qi_p, sm_p, kir_p = _front_idx(x_prompt, mod_p, tab_p, gn, widx, False)
    oa_p = _dsa_p(qi_p, sm_p, kir_p, qa_p, kab_p, vt_p, min(TOPK_MAX, t // 4))
    ob_p = _mla_p(qn_p, qp_p, latb_p, krr_p, wkt, wvp)
    y_prompt = _back(x_prompt, mod_p, oa_p, za_p, ob_p, zb_p, ga_p, gb_p, woa, wob, wout, gf, False)

    tab_s = jnp.tile(_rope_table(n_past + jnp.arange(ts)), (db, 1))
    xs = x_sample.reshape(1, ns, D_MODEL)
    fs = _front(xs, mod_s, tab_s, gn, gq, gkv, win, wuq, True)
    (qa_s, ka_s, _, va_s, _, za_s, _, qn_s, qp_s, lat_s, _, zb_s, ga_s, gb_s) = [a[0] for a in fs]
    qi_s, sm_s, _ = [a[0] for a in _front_idx(xs, mod_s, tab_s, gn, widx, True)]

    def pad_to(a, axis, n):
        pads = [(0, 0)] * a.ndim
        pads[axis] = (0, n - a.shape[axis])
        return jnp.pad(a, pads)

    def new_keys_t(a):
        return pad_to(a.reshape(db, ts, -1).transpose(0, 2, 1), 2, LANES)

    def head_rows(a, nh, w):
        a = a.reshape(db, ts, nh, w).transpose(0, 2, 1, 3)
        return pad_to(a, 2, ROWS_S).reshape(db, nh * ROWS_S, w)

    pt_t = page_table.T
    cache_kt = cache_k_a.transpose(0, 2, 3, 1).reshape(n_pool, D_A, PAGE_SIZE)
    cache_vt = cache_v_a.transpose(0, 2, 3, 1).reshape(n_pool, D_A, PAGE_SIZE)
    cache_idx_kt = cache_idx_k.transpose(0, 2, 1)
    cache_kpet = cache_kpe.transpose(0, 2, 1)
    gp = _pages_per_step(n_pages, PAGES_PER_STEP)

    k_sel_s = min(TOPK_MAX, (n_past + ts) // 4)
    q_stack = head_rows(qi_s, H_IDX, D_IDX)
    w_i = sm_s[:, 96:96 + H_IDX] * (D_IDX ** -0.5)
    w_stack = jnp.broadcast_to(head_rows(w_i, H_IDX, 1), (db, H_IDX * ROWS_S, LANES))
    keys = _sc_s(pt_t, q_stack, w_stack, new_keys_t(sm_s[:, :D_IDX]), cache_idx_kt, ts, gp)
    mask = _sel_s(keys, k_sel_s)

    qa_pad = pad_to(qa_s.reshape(db, ts, D_A), 1, ROWS_S)
    oa_s = _dsa_s(pt_t, qa_pad, mask, new_keys_t(ka_s), new_keys_t(va_s), cache_kt, cache_vt, gp)
    oa_s = oa_s[:, :ts].reshape(1, ns, D_A)

    qn_pad = pad_to(qn_s.reshape(db, ts, D_A), 1, ROWS_S)
    qpe_rows = head_rows(qp_s, H_B, D_ROPE)
    lat_new = pad_to(lat_s.reshape(db, ts, D_C), 1, PAGE_SIZE)
    ob_s = _mla_s(pt_t, qn_pad, qpe_rows, lat_new, new_keys_t(sm_s[:, D_IDX:D_IDX + D_ROPE]), wkt, wvp,
                  cache_lat, cache_kpet, ts, gp)
    ob_s = ob_s[:, :ts].reshape(1, ns, D_B)

    y_sample = _back(xs, mod_s, oa_s, za_s[None], ob_s, zb_s[None], ga_s[None], gb_s[None],
                     woa, wob, wout, gf, True).reshape(db, ts, D_MODEL)

    return (y_prompt, y_sample,
            ka_p.reshape(b, t, H_A, D_HEAD_A), va_p.reshape(b, t, H_A, D_HEAD_A),
            sm_p[:, :, :D_IDX], lat_p, sm_p[:, :, D_IDX:D_IDX + D_ROPE],
            ka_s.reshape(db, ts, H_A, D_HEAD_A), va_s.reshape(db, ts, H_A, D_HEAD_A),
            sm_s[:, :D_IDX].reshape(db, ts, D_IDX), lat_s.reshape(db, ts, D_C),
            sm_s[:, D_IDX:D_IDX + D_ROPE].reshape(db, ts, D_ROPE))
```
